```python
import math
import jax, jax.numpy as jnp
from jax import lax
import numpy as np

D_MODEL = 2048
BATCH = 2
SEQ = 4096
DEPTH = 2
DEC_BATCH = 128
DEC_SEQ = 8
PAST_LEN = 2048
PAGE_SIZE = 128

HEAD_DIM = 128
H_SB = 4
H_FOX = 6
H_DIFF = 6
N_HEADS = H_SB + H_FOX + H_DIFF
MIX_W = N_HEADS * HEAD_DIM
DIFF_QK = HEAD_DIM // 2
D_FF = 4 * D_MODEL
IN_W = 3 * MIX_W + H_FOX
Q_BLOCK = 128
EPS = 1e-6
FOX_BIAS = 3.0

kernel_name = "hybrid_sb_fox_diff_decoder_step"


def rms_norm(x, g):
    xf = x.astype(jnp.float32)
    y = xf * lax.rsqrt(jnp.mean(xf * xf, axis=-1, keepdims=True) + EPS)
    return (y * g.astype(jnp.float32)).astype(x.dtype)


def qk_norm(t, g_fox, g_diff):
    B, T = t.shape[:2]
    t_sb = t[:, :, :H_SB]
    t_fx = rms_norm(t[:, :, H_SB:H_SB + H_FOX], g_fox)
    t_df = rms_norm(t[:, :, H_SB + H_FOX:].reshape(B, T, H_DIFF, 2, DIFF_QK), g_diff)
    return jnp.concatenate([t_sb, t_fx, t_df.reshape(B, T, H_DIFF, HEAD_DIM)], axis=2)


def project(h, w_in_l, b_f_l, g_qf, g_kf, g_qd, g_kd):
    B, T, _ = h.shape
    z = h @ w_in_l
    q = z[..., :MIX_W].reshape(B, T, N_HEADS, HEAD_DIM)
    k = z[..., MIX_W:2 * MIX_W].reshape(B, T, N_HEADS, HEAD_DIM)
    v = z[..., 2 * MIX_W:3 * MIX_W].reshape(B, T, N_HEADS, HEAD_DIM)
    logf = jax.nn.log_sigmoid((z[..., 3 * MIX_W:] + b_f_l).astype(jnp.float32))
    return qk_norm(q, g_qf, g_qd), qk_norm(k, g_kf, g_kd), v, logf


def stick_breaking(q, k, v, strict):
    z = jnp.einsum('bhqd,bhkd->bhqk', q, k, preferred_element_type=jnp.float32) * (HEAD_DIM ** -0.5)
    log_1m = jnp.where(strict, jax.nn.log_sigmoid(-z), 0.0)
    tail = lax.cumsum(log_1m, axis=3, reverse=True) - log_1m
    w = jnp.where(strict, jnp.exp(jax.nn.log_sigmoid(z) + tail), 0.0)
    return jnp.einsum('bhqk,bhkd->bhqd', w.astype(v.dtype), v)


def forgetting(q, k, v, fq, fk, causal):
    s = jnp.einsum('bhqd,bhkd->bhqk', q, k, preferred_element_type=jnp.float32) * (HEAD_DIM ** -0.5)
    s = s + fq[..., :, None] - fk[..., None, :]
    p = jax.nn.softmax(jnp.where(causal, s, -jnp.inf), axis=-1)
    return jnp.einsum('bhqk,bhkd->bhqd', p.astype(v.dtype), v)


def differential(q, k, v, dist, causal, lam, lam_init, g_sub):
    slopes = jnp.exp2(-8.0 * jnp.arange(1, H_DIFF + 1, dtype=jnp.float32) / H_DIFF)
    bias = -slopes[:, None, None] * dist

    def smap(qa, ka):
        s = jnp.einsum('bhqd,bhkd->bhqk', qa, ka, preferred_element_type=jnp.float32) * (DIFF_QK ** -0.5) + bias
        return jax.nn.softmax(jnp.where(causal, s, -jnp.inf), axis=-1)

    p = smap(q[..., :DIFF_QK], k[..., :DIFF_QK]) - lam * smap(q[..., DIFF_QK:], k[..., DIFF_QK:])
    o = jnp.einsum('bhqk,bhkd->bhqd', p.astype(v.dtype), v)
    return rms_norm(o, g_sub) * (1.0 - lam_init)


def attend(q, k, v, fq, fk, qpos, kpos, lam, lam_init, g_sub):
    a, b = H_SB, H_SB + H_FOX
    causal = kpos[None, :] <= qpos[:, None]
    strict = kpos[None, :] < qpos[:, None]
    dist = (qpos[:, None] - kpos[None, :]).astype(jnp.float32)
    o_sb = stick_breaking(q[:, :a], k[:, :a], v[:, :a], strict)
    o_fx = forgetting(q[:, a:b], k[:, a:b], v[:, a:b], fq, fk, causal)
    o_df = differential(q[:, b:], k[:, b:], v[:, b:], dist, causal, lam, lam_init, g_sub)
    o = jnp.concatenate([o_sb, o_fx, o_df], axis=1)
    B, H, Tq, d = o.shape
    return o.transpose(0, 2, 1, 3).reshape(B, Tq, H * d)


def prompt_mix(q, k, v, logf, lam, lam_init, g_sub):
    B, T = q.shape[:2]
    qh, kh, vh = q.transpose(0, 2, 1, 3), k.transpose(0, 2, 1, 3), v.transpose(0, 2, 1, 3)
    F = jnp.cumsum(logf, axis=1).transpose(0, 2, 1)
    kpos = jnp.arange(T)

    def block(i):
        s0 = i * Q_BLOCK
        qb = lax.dynamic_slice_in_dim(qh, s0, Q_BLOCK, axis=2)
        fb = lax.dynamic_slice_in_dim(F, s0, Q_BLOCK, axis=2)
        return attend(qb, kh, vh, fb, F, s0 + jnp.arange(Q_BLOCK), kpos, lam, lam_init, g_sub)

    o = lax.map(block, jnp.arange(T // Q_BLOCK))
    return o.transpose(1, 0, 2, 3).reshape(B, T, MIX_W)


def sample_mix(q, k_all, v_all, logf_all, past_len, lam, lam_init, g_sub):
    Tq = q.shape[1]
    Tk = k_all.shape[1]
    F = jnp.cumsum(logf_all, axis=1).transpose(0, 2, 1)
    qpos = past_len + jnp.arange(Tq)
    kpos = jnp.arange(Tk)
    return attend(q.transpose(0, 2, 1, 3), k_all.transpose(0, 2, 1, 3), v_all.transpose(0, 2, 1, 3),
                  F[:, :, past_len:], F, qpos, kpos, lam, lam_init, g_sub)


def mlp(h, w_up_l, w_down_l):
    return jnp.square(jax.nn.relu(h @ w_up_l)) @ w_down_l


def setup_inputs(seed: int = 0) -> dict:
    key = jax.random.key(seed)
    ks = jax.random.split(key, 20)
    n_pages = PAST_LEN // PAGE_SIZE
    n_used = DEC_BATCH * n_pages
    n_phys = n_used + n_used // 4
    nrm = jax.random.normal
    page_table = jax.random.permutation(ks[5], n_phys)[:n_used].reshape(DEC_BATCH, n_pages).astype(jnp.int32)
    return {
        "x_prompt": nrm(ks[0], (BATCH, SEQ, D_MODEL), jnp.float32),
        "x_sample": nrm(ks[1], (DEC_BATCH, DEC_SEQ, D_MODEL), jnp.float32),
        "cache_k": nrm(ks[2], (DEPTH, n_phys, PAGE_SIZE, N_HEADS, HEAD_DIM), jnp.float32),
        "cache_v": nrm(ks[3], (DEPTH, n_phys, PAGE_SIZE, N_HEADS, HEAD_DIM), jnp.float32),
        "cache_logf": jax.nn.log_sigmoid(FOX_BIAS + nrm(ks[4], (DEPTH, n_phys, PAGE_SIZE, H_FOX), jnp.float32)),
        "page_table": page_table,
        "ln1_g": 1.0 + 0.02 * nrm(ks[6], (DEPTH, D_MODEL), jnp.float32),
        "w_in": nrm(ks[7], (DEPTH, D_MODEL, IN_W), jnp.float32) * D_MODEL ** -0.5,
        "b_f": FOX_BIAS + 0.1 * nrm(ks[8], (DEPTH, H_FOX), jnp.float32),
        "g_q_fox": 1.0 + 0.02 * nrm(ks[9], (DEPTH, HEAD_DIM), jnp.float32),
        "g_k_fox": 1.0 + 0.02 * nrm(ks[10], (DEPTH, HEAD_DIM), jnp.float32),
        "g_q_diff": 1.0 + 0.02 * nrm(ks[11], (DEPTH, 2, DIFF_QK), jnp.float32),
        "g_k_diff": 1.0 + 0.02 * nrm(ks[12], (DEPTH, 2, DIFF_QK), jnp.float32),
        "diff_lambda": 0.1 * nrm(ks[13], (DEPTH, 4, DIFF_QK), jnp.float32),
        "g_sub_diff": 1.0 + 0.02 * nrm(ks[14], (DEPTH, HEAD_DIM), jnp.float32),
        "w_out": nrm(ks[15], (DEPTH, MIX_W, D_MODEL), jnp.float32) * MIX_W ** -0.5,
        "ln2_g": 1.0 + 0.02 * nrm(ks[16], (DEPTH, D_MODEL), jnp.float32),
        "w_up": nrm(ks[17], (DEPTH, D_MODEL, D_FF), jnp.float32) * D_MODEL ** -0.5,
        "w_down": nrm(ks[18], (DEPTH, D_FF, D_MODEL), jnp.float32) * D_FF ** -0.5,
    }


def reference(x_prompt, x_sample, cache_k, cache_v, cache_logf, page_table, ln1_g, w_in, b_f,
              g_q_fox, g_k_fox, g_q_diff, g_k_diff, diff_lambda, g_sub_diff, w_out, ln2_g, w_up, w_down):
    db, n_pages = page_table.shape
    past_len = n_pages * PAGE_SIZE
    xp, xs = x_prompt, x_sample
    kp_l, vp_l, fp_l, ks_l, vs_l, fs_l = [], [], [], [], [], []
    for l in range(DEPTH):
        lam_init = 0.8 - 0.6 * math.exp(-0.3 * l)
        lv = diff_lambda[l].astype(jnp.float32)
        lam = jnp.exp(jnp.dot(lv[0], lv[1])) - jnp.exp(jnp.dot(lv[2], lv[3])) + lam_init
        pargs = (w_in[l], b_f[l], g_q_fox[l], g_k_fox[l], g_q_diff[l], g_k_diff[l])

        q, k, v, logf = project(rms_norm(xp, ln1_g[l]), *pargs)
        xp = xp + prompt_mix(q, k, v, logf, lam, lam_init, g_sub_diff[l]) @ w_out[l]
        xp = xp + mlp(rms_norm(xp, ln2_g[l]), w_up[l], w_down[l])
        kp_l.append(k); vp_l.append(v); fp_l.append(logf)

        q, k, v, logf = project(rms_norm(xs, ln1_g[l]), *pargs)
        past_k = cache_k[l][page_table].reshape(db, past_len, N_HEADS, HEAD_DIM)
        past_v = cache_v[l][page_table].reshape(db, past_len, N_HEADS, HEAD_DIM)
        past_f = cache_logf[l][page_table].reshape(db, past_len, H_FOX)
        k_all = jnp.concatenate([past_k.astype(k.dtype), k], axis=1)
        v_all = jnp.concatenate([past_v.astype(v.dtype), v], axis=1)
        f_all = jnp.concatenate([past_f.astype(jnp.float32), logf], axis=1)
        xs = xs + sample_mix(q, k_all, v_all, f_all, past_len, lam, lam_init, g_sub_diff[l]) @ w_out[l]
        xs = xs + mlp(rms_norm(xs, ln2_g[l]), w_up[l], w_down[l])
        ks_l.append(k); vs_l.append(v); fs_l.append(logf)

    return (xp, xs, jnp.stack(kp_l), jnp.stack(vp_l), jnp.stack(fp_l),
            jnp.stack(ks_l), jnp.stack(vs_l), jnp.stack(fs_l))
```

```python
import functools
import math

import jax
import jax.numpy as jnp
from jax import lax
from jax.experimental import pallas as pl
from jax.experimental.pallas import tpu as pltpu

HEAD_DIM = 128
H_SB = 4
H_FOX = 6
H_DIFF = 6
N_HEADS = H_SB + H_FOX + H_DIFF
MIX_W = N_HEADS * HEAD_DIM
DIFF_QK = HEAD_DIM // 2
EPS = 1e-6
NEG = -1e30
ALIBI_RATE = 8.0 / H_DIFF

DEC_Q = 8
SB_ROWS = H_SB * DEC_Q
FOX_ROWS = H_FOX * DEC_Q
DIFF_ROWS = H_DIFF * 2 * DEC_Q
FOX_ROW0 = SB_ROWS
DIFF_ROW0 = SB_ROWS + FOX_ROWS
ALL_ROWS = SB_ROWS + FOX_ROWS + DIFF_ROWS

VMEM_LIMIT = 56 * 1024 * 1024

f32 = jnp.float32
bf16 = jnp.bfloat16


def _dot(a, b):
    return jnp.dot(a, b, preferred_element_type=f32)


def _dot_nt(a, b):
    return lax.dot_general(a, b, (((1,), (1,)), ((), ())), preferred_element_type=f32)


def _dot_exact(a, b):
    return jnp.dot(a, b, preferred_element_type=f32, precision=lax.Precision.HIGHEST)


def _log_sigmoid(x):
    return jnp.minimum(x, 0.0) - jnp.log1p(jnp.exp(-jnp.abs(x)))


def _log_one_minus_sigmoid(z):
    return -(jnp.maximum(z, 0.0) + jnp.log1p(jnp.exp(-jnp.abs(z))))


def _split_bf16(x):
    hi = x.astype(bf16)
    lo = (x - hi.astype(f32)).astype(bf16)
    return hi, lo


def _lam(lam_ref, lam_init):
    lv = lam_ref[...]
    d1 = jnp.sum(lv[0:1, :] * lv[1:2, :], axis=1, keepdims=True)
    d2 = jnp.sum(lv[2:3, :] * lv[3:4, :], axis=1, keepdims=True)
    return jnp.exp(d1) - jnp.exp(d2) + lam_init


def _diff_finish(o1, o2, lam, gsub, lam_init):
    o = o1 - lam * o2
    ms = jnp.mean(o * o, axis=-1, keepdims=True)
    return o * lax.rsqrt(ms + EPS) * gsub * (1.0 - lam_init)


def _norm_heads(z, g, group):
    parts = []
    for hh in range(z.shape[1] // HEAD_DIM):
        zz = z[:, hh * HEAD_DIM:(hh + 1) * HEAD_DIM]
        z2 = zz * zz
        if group == HEAD_DIM:
            ms = jnp.mean(z2, axis=-1, keepdims=True)
        else:
            lo = lax.broadcasted_iota(jnp.int32, (1, HEAD_DIM), 1) < DIFF_QK
            s_lo = jnp.sum(jnp.where(lo, z2, 0.0), axis=-1, keepdims=True)
            s_hi = jnp.sum(jnp.where(lo, 0.0, z2), axis=-1, keepdims=True)
            ms = jnp.where(lo, s_lo, s_hi) * (1.0 / DIFF_QK)
        parts.append(zz * lax.rsqrt(ms + EPS) * g[:, hh * HEAD_DIM:(hh + 1) * HEAD_DIM])
    return jnp.concatenate(parts, axis=1)


def _in_proj_kernel(x_ref, g1_ref, wq_ref, wk_ref, wv_ref, wf_ref, bf_ref, gq_ref, gk_ref,
                    q_ref, k_ref, v_ref, lf_ref, *rest, emit_bf16_kv):
    if emit_bf16_kv:
        kb_ref, vb_ref, h_scr = rest
    else:
        (h_scr,) = rest
    j = pl.program_id(1)

    @pl.when(j == 0)
    def _():
        x = x_ref[...]
        ms = jnp.mean(x * x, axis=-1, keepdims=True)
        h_scr[...] = (x * lax.rsqrt(ms + EPS) * g1_ref[...]).astype(bf16)
        lf_ref[...] = _log_sigmoid(_dot(h_scr[...], wf_ref[...]) + bf_ref[...])

    h = h_scr[...]
    zq = _dot(h, wq_ref[...])
    zk = _dot(h, wk_ref[...])
    zv = _dot(h, wv_ref[...])
    v_ref[...] = zv
    if emit_bf16_kv:
        vb_ref[...] = zv.astype(bf16)

    def store(q, k):
        q_ref[...] = q.astype(q_ref.dtype)
        k_ref[...] = k
        if emit_bf16_kv:
            kb_ref[...] = k.astype(bf16)

    @pl.when(j < H_SB // 2)
    def _():
        store(zq * HEAD_DIM ** -0.5, zk)

    @pl.when((j >= H_SB // 2) & (j < (H_SB + H_FOX) // 2))
    def _():
        store(_norm_heads(zq, gq_ref[...], HEAD_DIM) * HEAD_DIM ** -0.5, _norm_heads(zk, gk_ref[...], HEAD_DIM))

    @pl.when(j >= (H_SB + H_FOX) // 2)
    def _():
        store(_norm_heads(zq, gq_ref[...], DIFF_QK) * DIFF_QK ** -0.5, _norm_heads(zk, gk_ref[...], DIFF_QK))


def _in_proj(x, g1, wq, wk, wv, wf, bfp, gq, gk, *, q_dtype, emit_bf16_kv):
    m, d = x.shape
    tm = min(512, m)
    tn = 2 * HEAD_DIM
    row = lambda i, j: (i, 0)
    col = lambda i, j: (0, j)
    tile = lambda i, j: (i, j)
    const = lambda i, j: (0, 0)
    out_shape = [jax.ShapeDtypeStruct((m, MIX_W), q_dtype),
                 jax.ShapeDtypeStruct((m, MIX_W), f32),
                 jax.ShapeDtypeStruct((m, MIX_W), f32),
                 jax.ShapeDtypeStruct((m, HEAD_DIM), f32)]
    out_specs = [pl.BlockSpec((tm, tn), tile), pl.BlockSpec((tm, tn), tile), pl.BlockSpec((tm, tn), tile),
                 pl.BlockSpec((tm, HEAD_DIM), row)]
    if emit_bf16_kv:
        out_shape += [jax.ShapeDtypeStruct((m, MIX_W), bf16)] * 2
        out_specs += [pl.BlockSpec((tm, tn), tile)] * 2
    return pl.pallas_call(
        functools.partial(_in_proj_kernel, emit_bf16_kv=emit_bf16_kv),
        grid=(m // tm, MIX_W // tn),
        in_specs=[pl.BlockSpec((tm, d), row), pl.BlockSpec((1, d), const),
                  pl.BlockSpec((d, tn), col), pl.BlockSpec((d, tn), col), pl.BlockSpec((d, tn), col),
                  pl.BlockSpec((d, HEAD_DIM), const), pl.BlockSpec((1, HEAD_DIM), const),
                  pl.BlockSpec((1, tn), col), pl.BlockSpec((1, tn), col)],
        out_specs=out_specs,
        out_shape=out_shape,
        scratch_shapes=[pltpu.VMEM((tm, d), bf16)],
        compiler_params=pltpu.CompilerParams(dimension_semantics=("arbitrary", "arbitrary"),
                                             vmem_limit_bytes=VMEM_LIMIT),
        name="in_proj",
    )(x, g1, wq, wk, wv, wf, bfp, gq, gk)


def _cumsum_kernel(lf_ref, ft_ref, carry_scr):
    @pl.when(pl.program_id(1) == 0)
    def _():
        carry_scr[...] = jnp.zeros_like(carry_scr)

    tc = lf_ref.shape[0]
    lf_t = lf_ref[...].T[:8]
    r = lax.broadcasted_iota(jnp.int32, (tc, tc), 0)
    c = lax.broadcasted_iota(jnp.int32, (tc, tc), 1)
    cs = _dot_exact(lf_t, (r <= c).astype(f32)) + carry_scr[:, 0:1]
    ft_ref[...] = cs
    carry_scr[...] = jnp.broadcast_to(cs[:, tc - 1:tc], carry_scr.shape)


def _cumsum_t(lf, batch, t):
    tc = min(512, t)
    nblk = t // tc
    return pl.pallas_call(
        _cumsum_kernel,
        grid=(batch, nblk),
        in_specs=[pl.BlockSpec((tc, HEAD_DIM), lambda b, i: (b * nblk + i, 0))],
        out_specs=pl.BlockSpec((None, 8, tc), lambda b, i: (b, 0, i)),
        out_shape=jax.ShapeDtypeStruct((batch, 8, t), f32),
        scratch_shapes=[pltpu.VMEM((8, HEAD_DIM), f32)],
        compiler_params=pltpu.CompilerParams(dimension_semantics=("arbitrary", "arbitrary")),
        name="logf_cumsum",
    )(lf)


def _prompt_attn_kernel(q_ref, k_ref, v_ref, ft_ref, lam_ref, gsub_ref, o_ref, *, tq, lam_init):
    h = pl.program_id(1)
    qi = pl.program_id(2)
    q = q_ref[...]
    row = lax.broadcasted_iota(jnp.int32, (tq, tq), 0)
    col = lax.broadcasted_iota(jnp.int32, (tq, tq), 1)

    def kv(kb):
        off = pl.multiple_of(kb * tq, tq)
        return k_ref[pl.ds(off, tq), :], v_ref[pl.ds(off, tq), :], off

    @pl.when(h < H_SB)
    def _():
        after = (row > col).astype(bf16)
        strict = col < row

        def block(kb, carry, acc, masked):
            k, v, _ = kv(kb)
            z = _dot_nt(q, k)
            lg = _log_one_minus_sigmoid(z)
            if masked:
                lg = jnp.where(strict, lg, 0.0)
            hi, lo = _split_bf16(lg)
            tail = _dot(hi, after) + _dot(lo, after)
            w = jnp.exp(z + lg + tail + carry)
            if masked:
                w = jnp.where(strict, w, 0.0)
            acc = acc + _dot(w.astype(bf16), v)
            carry = carry + jnp.sum(lg, axis=1, keepdims=True)
            return carry, acc

        state = block(qi, jnp.zeros((tq, 1), f32), jnp.zeros((tq, HEAD_DIM), f32), True)
        _, acc = lax.fori_loop(0, qi, lambda i, s: block(qi - 1 - i, s[0], s[1], False), state)
        o_ref[...] = acc.astype(o_ref.dtype)

    def softmax_block(s, v, m, l, acc):
        m_new = jnp.maximum(m, jnp.max(s, axis=1, keepdims=True))
        alpha = jnp.exp(m - m_new)
        p = jnp.exp(s - m_new)
        l = alpha * l + jnp.sum(p, axis=1, keepdims=True)
        acc = alpha * acc + _dot(p.astype(bf16), v)
        return m_new, l, acc

    @pl.when((h >= H_SB) & (h < H_SB + H_FOX))
    def _():
        fx = h - H_SB

        def block(kb, m, l, acc, masked):
            k, v, off = kv(kb)
            s = _dot_nt(q, k) - ft_ref[pl.ds(fx, 1), pl.ds(off, tq)]
            if masked:
                s = jnp.where(col <= row, s, NEG)
            return softmax_block(s, v, m, l, acc)

        state = (jnp.full((tq, 1), NEG, f32), jnp.zeros((tq, 1), f32), jnp.zeros((tq, HEAD_DIM), f32))
        state = lax.fori_loop(0, qi, lambda kb, s: block(kb, *s, False), state)
        _, l, acc = block(qi, *state, True)
        o_ref[...] = (acc / l).astype(o_ref.dtype)

    @pl.when(h >= H_SB + H_FOX)
    def _():
        hd = h - (H_SB + H_FOX)
        slope = jnp.exp2(jnp.full((1, 1), -ALIBI_RATE, f32) * (hd + 1).astype(f32))
        lane = lax.broadcasted_iota(jnp.int32, (tq, HEAD_DIM), 1)
        zero = jnp.zeros_like(q)
        q2 = jnp.concatenate([jnp.where(lane < DIFF_QK, q, zero), jnp.where(lane < DIFF_QK, zero, q)], axis=0)
        row2 = lax.broadcasted_iota(jnp.int32, (2 * tq, tq), 0) & (tq - 1)
        col2 = lax.broadcasted_iota(jnp.int32, (2 * tq, tq), 1)
        kidx = lax.broadcasted_iota(jnp.int32, (1, tq), 1).astype(f32)

        def block(kb, m, l, acc, masked):
            k, v, _ = kv(kb)
            kpos = kidx + ((kb - qi) * tq).astype(f32)
            s = _dot_nt(q2, k) + slope * kpos
            if masked:
                s = jnp.where(col2 <= row2, s, NEG)
            return softmax_block(s, v, m, l, acc)

        state = (jnp.full((2 * tq, 1), NEG, f32), jnp.zeros((2 * tq, 1), f32),
                 jnp.zeros((2 * tq, HEAD_DIM), f32))
        state = lax.fori_loop(0, qi, lambda kb, s: block(kb, *s, False), state)
        _, l, acc = block(qi, *state, True)
        o = acc / l
        out = _diff_finish(o[:tq], o[tq:], _lam(lam_ref, lam_init), gsub_ref[...], lam_init)
        o_ref[...] = out.astype(o_ref.dtype)


def _prompt_attn(q, k, v, ft, lam_p, gsub, *, batch, t, lam_init):
    tq = min(256, t)
    nq = t // tq
    full = lambda b, h, i: (0, 0)
    return pl.pallas_call(
        functools.partial(_prompt_attn_kernel, tq=tq, lam_init=lam_init),
        grid=(batch, N_HEADS, nq),
        in_specs=[pl.BlockSpec((tq, HEAD_DIM), lambda b, h, i: (b * nq + i, h)),
                  pl.BlockSpec((t, HEAD_DIM), lambda b, h, i: (b, h)),
                  pl.BlockSpec((t, HEAD_DIM), lambda b, h, i: (b, h)),
                  pl.BlockSpec((None, 8, t), lambda b, h, i: (b, 0, 0)),
                  pl.BlockSpec(lam_p.shape, full), pl.BlockSpec(gsub.shape, full)],
        out_specs=pl.BlockSpec((tq, HEAD_DIM), lambda b, h, i: (b * nq + i, h)),
        out_shape=jax.ShapeDtypeStruct((batch * t, MIX_W), bf16),
        compiler_params=pltpu.CompilerParams(dimension_semantics=("arbitrary", "arbitrary", "arbitrary"),
                                             vmem_limit_bytes=VMEM_LIMIT),
        name="prompt_attn",
    )(q, k, v, ft, lam_p, gsub)


def _sample_attn_kernel(pt_ref, q_ref, kn_ref, vn_ref, lfn_ref, *rest, pages_per_step, n_groups, past_len,
                        lam_init):
    pp = pages_per_step
    k_refs = rest[:pp]
    v_refs = rest[pp:2 * pp]
    lf_refs = rest[2 * pp:3 * pp]
    lam_ref, gsub_ref, o_ref, qbd_scr, acc_scr, m_scr, l_scr, hc_scr = rest[3 * pp:]
    g = pl.program_id(1)
    page = lf_refs[0].shape[1]

    col = lax.broadcasted_iota(jnp.int32, (1, page), 1)
    r_c = lax.broadcasted_iota(jnp.int32, (page, page), 0)
    c_c = lax.broadcasted_iota(jnp.int32, (page, page), 1)
    after_bf = (r_c > c_c).astype(bf16)
    after_f32 = (r_c > c_c).astype(f32)

    def qidx(rows):
        return lax.broadcasted_iota(jnp.int32, (rows, 1), 0) & (DEC_Q - 1)

    def process(k_f32, v_f32, lf, kpos0, is_new):
        kb = k_f32.astype(bf16)
        vb = v_f32.astype(bf16)
        s_all = _dot_nt(qbd_scr[...], kb)

        z = s_all[:SB_ROWS]
        lg = _log_one_minus_sigmoid(z)
        if is_new:
            sb_ok = col < qidx(SB_ROWS)
            lg = jnp.where(sb_ok, lg, 0.0)
        hi, lo = _split_bf16(lg)
        tail = _dot(hi, after_bf) + _dot(lo, after_bf)
        w = jnp.exp(z + lg + tail + m_scr[:SB_ROWS])
        if is_new:
            w = jnp.where(sb_ok, w, 0.0)
        m_scr[:SB_ROWS] = m_scr[:SB_ROWS] + jnp.sum(lg, axis=1, keepdims=True)

        later = _dot_exact(lf, after_f32) + hc_scr[:, 0:1]
        hc_scr[...] = hc_scr[...] + jnp.sum(lf, axis=1, keepdims=True)
        later_rows = jnp.concatenate(
            [jnp.broadcast_to(later[f:f + 1], (DEC_Q, page)) for f in range(H_FOX)], axis=0)
        s_fx = s_all[FOX_ROW0:DIFF_ROW0] + later_rows

        head = lax.broadcasted_iota(jnp.int32, (DIFF_ROWS, 1), 0) // (2 * DEC_Q)
        slope = jnp.exp2(-ALIBI_RATE * (head + 1).astype(f32))
        s_df = s_all[DIFF_ROW0:] + slope * (col.astype(f32) + kpos0)

        s = jnp.concatenate([s_fx, s_df], axis=0)
        if is_new:
            s = jnp.where(col <= qidx(FOX_ROWS + DIFF_ROWS), s, NEG)
        m_old = m_scr[FOX_ROW0:]
        m_new = jnp.maximum(m_old, jnp.max(s, axis=1, keepdims=True))
        alpha = jnp.exp(m_old - m_new)
        p = jnp.exp(s - m_new)
        m_scr[FOX_ROW0:] = m_new
        l_scr[FOX_ROW0:] = alpha * l_scr[FOX_ROW0:] + jnp.sum(p, axis=1, keepdims=True)

        pw = jnp.concatenate([w, p], axis=0).astype(bf16)
        pieces = []
        for c in range(N_HEADS // 2):
            h0 = 2 * c
            if h0 < H_SB + H_FOX:
                r0, per = h0 * DEC_Q, DEC_Q
            else:
                r0, per = DIFF_ROW0 + (h0 - H_SB - H_FOX) * 2 * DEC_Q, 2 * DEC_Q
            res = _dot(pw[r0:r0 + 2 * per], vb[:, c * 2 * HEAD_DIM:(c + 1) * 2 * HEAD_DIM])
            pieces.append(res[:per, :HEAD_DIM])
            pieces.append(res[per:, HEAD_DIM:])
        pv = jnp.concatenate(pieces, axis=0)
        scale = jnp.concatenate([jnp.ones((SB_ROWS, 1), f32), alpha], axis=0)
        acc_scr[...] = scale * acc_scr[...] + pv

    @pl.when(g == 0)
    def _():
        q = q_ref[...]
        lane = lax.broadcasted_iota(jnp.int32, (DEC_Q, HEAD_DIM), 1)
        rows = []
        for hh in range(N_HEADS):
            tile = q[:, hh * HEAD_DIM:(hh + 1) * HEAD_DIM]
            if hh < H_SB + H_FOX:
                tiles = [tile]
            else:
                tiles = [jnp.where(lane < DIFF_QK, tile, 0.0), jnp.where(lane < DIFF_QK, 0.0, tile)]
            for tl in tiles:
                parts = []
                if hh > 0:
                    parts.append(jnp.zeros((DEC_Q, hh * HEAD_DIM), f32))
                parts.append(tl)
                if hh < N_HEADS - 1:
                    parts.append(jnp.zeros((DEC_Q, (N_HEADS - 1 - hh) * HEAD_DIM), f32))
                rows.append(jnp.concatenate(parts, axis=1))
        qbd_scr[...] = jnp.concatenate(rows, axis=0).astype(bf16)
        acc_scr[...] = jnp.zeros_like(acc_scr)
        l_scr[...] = jnp.zeros_like(l_scr)
        hc_scr[...] = jnp.zeros_like(hc_scr)
        m_scr[...] = jnp.concatenate(
            [jnp.zeros((SB_ROWS, 1), f32), jnp.full((FOX_ROWS + DIFF_ROWS, 1), NEG, f32)], axis=0)
        pad = jnp.zeros((page - DEC_Q, MIX_W), f32)
        process(jnp.concatenate([kn_ref[...], pad], axis=0), jnp.concatenate([vn_ref[...], pad], axis=0),
                lfn_ref[...], 0.0, True)

    @pl.when(g > 0)
    def _():
        first_page = (n_groups - g) * pp
        def load_page(ref):
            return jnp.concatenate([ref[pl.ds(hh, page, stride=N_HEADS), :] for hh in range(N_HEADS)], axis=1)

        for i in reversed(range(pp)):
            kpos0 = ((first_page + i) * page - past_len).astype(f32)
            process(load_page(k_refs[i]), load_page(v_refs[i]), lf_refs[i][...], kpos0, False)

    @pl.when(g == n_groups)
    def _():
        acc = acc_scr[...]
        l = l_scr[...]
        lam = _lam(lam_ref, lam_init)
        gsub = gsub_ref[...]
        outs = []
        for hh in range(N_HEADS):
            if hh < H_SB:
                outs.append(acc[hh * DEC_Q:(hh + 1) * DEC_Q])
            elif hh < H_SB + H_FOX:
                r0 = hh * DEC_Q
                outs.append(acc[r0:r0 + DEC_Q] / l[r0:r0 + DEC_Q])
            else:
                r0 = DIFF_ROW0 + (hh - H_SB - H_FOX) * 2 * DEC_Q
                o1 = acc[r0:r0 + DEC_Q] / l[r0:r0 + DEC_Q]
                o2 = acc[r0 + DEC_Q:r0 + 2 * DEC_Q] / l[r0 + DEC_Q:r0 + 2 * DEC_Q]
                outs.append(_diff_finish(o1, o2, lam, gsub, lam_init))
        o_ref[...] = jnp.concatenate(outs, axis=1)


def _sample_attn(page_table, q, kn, vn, lfn, cache_k, cache_v, cache_lf, lam_p, gsub, *, layer, lam_init):
    n_seq, n_pages = page_table.shape
    page = cache_lf.shape[3]
    pp = 4 if n_pages % 4 == 0 else (2 if n_pages % 2 == 0 else 1)
    n_groups = n_pages // pp
    assert q.shape[1] == DEC_Q and page == HEAD_DIM

    def page_map(i):
        def index(b, g, pt):
            return (layer, pt[b, (n_groups - jnp.maximum(g, 1)) * pp + i], 0, 0)
        return index

    seq = lambda b, g, pt: (b, 0, 0)
    full = lambda b, g, pt: (0, 0)
    in_specs = [pl.BlockSpec((None, DEC_Q, MIX_W), seq), pl.BlockSpec((None, DEC_Q, MIX_W), seq),
                pl.BlockSpec((None, DEC_Q, MIX_W), seq), pl.BlockSpec((None, 8, page), seq)]
    in_specs += [pl.BlockSpec((None, None, page * N_HEADS, HEAD_DIM), page_map(i)) for i in range(pp)]
    in_specs += [pl.BlockSpec((None, None, page * N_HEADS, HEAD_DIM), page_map(i)) for i in range(pp)]
    in_specs += [pl.BlockSpec((None, None, 8, page), page_map(i)) for i in range(pp)]
    in_specs += [pl.BlockSpec(lam_p.shape, full), pl.BlockSpec(gsub.shape, full)]
    grid_spec = pltpu.PrefetchScalarGridSpec(
        num_scalar_prefetch=1,
        grid=(n_seq, n_groups + 1),
        in_specs=in_specs,
        out_specs=pl.BlockSpec((None, DEC_Q, MIX_W), seq),
        scratch_shapes=[pltpu.VMEM((ALL_ROWS, MIX_W), bf16), pltpu.VMEM((ALL_ROWS, HEAD_DIM), f32),
                        pltpu.VMEM((ALL_ROWS, 1), f32), pltpu.VMEM((ALL_ROWS, 1), f32),
                        pltpu.VMEM((8, HEAD_DIM), f32)],
    )
    return pl.pallas_call(
        functools.partial(_sample_attn_kernel, pages_per_step=pp, n_groups=n_groups,
                          past_len=n_pages * page, lam_init=lam_init),
        grid_spec=grid_spec,
        out_shape=jax.ShapeDtypeStruct((n_seq, DEC_Q, MIX_W), f32),
        compiler_params=pltpu.CompilerParams(dimension_semantics=("arbitrary", "arbitrary"),
                                             vmem_limit_bytes=VMEM_LIMIT),
        name="sample_attn",
    )(page_table, q, kn, vn, lfn, *([cache_k] * pp), *([cache_v] * pp), *([cache_lf] * pp), lam_p, gsub)


def _out_proj_kernel(o_ref, w_ref, x_ref, y_ref):
    y_ref[...] = x_ref[...] + _dot(o_ref[...].astype(bf16), w_ref[...])


def _out_proj(o, w, x):
    m, d = x.shape
    tm = min(512, m)
    tn = min(1024, d)
    return pl.pallas_call(
        _out_proj_kernel,
        grid=(m // tm, d // tn),
        in_specs=[pl.BlockSpec((tm, o.shape[1]), lambda i, j: (i, 0)),
                  pl.BlockSpec((w.shape[0], tn), lambda i, j: (0, j)),
                  pl.BlockSpec((tm, tn), lambda i, j: (i, j))],
        out_specs=pl.BlockSpec((tm, tn), lambda i, j: (i, j)),
        out_shape=jax.ShapeDtypeStruct((m, d), f32),
        compiler_params=pltpu.CompilerParams(dimension_semantics=("arbitrary", "arbitrary"),
                                             vmem_limit_bytes=VMEM_LIMIT),
        name="out_proj",
    )(o, w, x)


def _mlp_kernel(x_ref, g2_ref, wu_ref, wd_ref, y_ref, h_scr):
    j = pl.program_id(1)

    @pl.when(j == 0)
    def _():
        x = x_ref[...]
        ms = jnp.mean(x * x, axis=-1, keepdims=True)
        h_scr[...] = (x * lax.rsqrt(ms + EPS) * g2_ref[...]).astype(bf16)
        y_ref[...] = x

    u = jnp.maximum(_dot(h_scr[...], wu_ref[...]), 0.0)
    y_ref[...] += _dot((u * u).astype(bf16), wd_ref[...])


def _mlp(x, g2, wu, wd):
    m, d = x.shape
    ff = wu.shape[1]
    tm = min(512, m)
    tf = min(1024, ff)
    return pl.pallas_call(
        _mlp_kernel,
        grid=(m // tm, ff // tf),
        in_specs=[pl.BlockSpec((tm, d), lambda i, j: (i, 0)), pl.BlockSpec((1, d), lambda i, j: (0, 0)),
                  pl.BlockSpec((d, tf), lambda i, j: (0, j)), pl.BlockSpec((tf, d), lambda i, j: (j, 0))],
        out_specs=pl.BlockSpec((tm, d), lambda i, j: (i, 0)),
        out_shape=jax.ShapeDtypeStruct((m, d), f32),
        scratch_shapes=[pltpu.VMEM((tm, d), bf16)],
        compiler_params=pltpu.CompilerParams(dimension_semantics=("arbitrary", "arbitrary"),
                                             vmem_limit_bytes=VMEM_LIMIT),
        name="mlp",
    )(x, g2, wu, wd)


def kernel(x_prompt, x_sample, cache_k, cache_v, cache_logf, page_table, ln1_g, w_in, b_f, g_q_fox, g_k_fox,
           g_q_diff, g_k_diff, diff_lambda, g_sub_diff, w_out, ln2_g, w_up, w_down):
    batch, t, d = x_prompt.shape
    n_seq, dec_q, _ = x_sample.shape
    depth, n_phys, page = cache_k.shape[:3]

    xp = x_prompt.reshape(batch * t, d)
    xs = x_sample.reshape(n_seq * dec_q, d)
    ck = cache_k.reshape(depth, n_phys, page * N_HEADS, HEAD_DIM)
    cv = cache_v.reshape(depth, n_phys, page * N_HEADS, HEAD_DIM)
    clf = jnp.pad(cache_logf.astype(f32).transpose(0, 1, 3, 2), ((0, 0), (0, 0), (0, 8 - H_FOX), (0, 0)))

    ones_sb = jnp.ones((H_SB * HEAD_DIM,), f32)
    outs = {name: [] for name in ("kp", "vp", "fp", "ks", "vs", "fs")}
    for l in range(depth):
        lam_init = 0.8 - 0.6 * math.exp(-0.3 * l)
        wq = w_in[l, :, :MIX_W].astype(bf16)
        wk = w_in[l, :, MIX_W:2 * MIX_W].astype(bf16)
        wv = w_in[l, :, 2 * MIX_W:3 * MIX_W].astype(bf16)
        wf = jnp.pad(w_in[l, :, 3 * MIX_W:], ((0, 0), (0, HEAD_DIM - H_FOX))).astype(bf16)
        bfp = jnp.pad(b_f[l].astype(f32), (0, HEAD_DIM - H_FOX)).reshape(1, HEAD_DIM)
        gq = jnp.concatenate([ones_sb, jnp.tile(g_q_fox[l], H_FOX), jnp.tile(g_q_diff[l].reshape(-1), H_DIFF)])
        gk = jnp.concatenate([ones_sb, jnp.tile(g_k_fox[l], H_FOX), jnp.tile(g_k_diff[l].reshape(-1), H_DIFF)])
        gq = gq.astype(f32).reshape(1, MIX_W)
        gk = gk.astype(f32).reshape(1, MIX_W)
        g1 = ln1_g[l].astype(f32).reshape(1, d)
        g2 = ln2_g[l].astype(f32).reshape(1, d)
        lam_p = diff_lambda[l].astype(f32)
        gsub = g_sub_diff[l].astype(f32).reshape(1, HEAD_DIM)
        wo = w_out[l].astype(bf16)
        wu = w_up[l].astype(bf16)
        wd = w_down[l].astype(bf16)

        q, k, v, lf, kb, vb = _in_proj(xp, g1, wq, wk, wv, wf, bfp, gq, gk, q_dtype=bf16, emit_bf16_kv=True)
        ft = _cumsum_t(lf, batch, t)
        o = _prompt_attn(q, kb, vb, ft, lam_p, gsub, batch=batch, t=t, lam_init=lam_init)
        xp = _mlp(_out_proj(o, wo, xp), g2, wu, wd)
        outs["kp"].append(k.reshape(batch, t, N_HEADS, HEAD_DIM))
        outs["vp"].append(v.reshape(batch, t, N_HEADS, HEAD_DIM))
        outs["fp"].append(lf[:, :H_FOX].reshape(batch, t, H_FOX))

        q, k, v, lf = _in_proj(xs, g1, wq, wk, wv, wf, bfp, gq, gk, q_dtype=f32, emit_bf16_kv=False)
        lfn = lf.reshape(n_seq, dec_q, HEAD_DIM)[:, :, :8].transpose(0, 2, 1)
        lfn = jnp.pad(lfn, ((0, 0), (0, 0), (0, page - dec_q)))
        lfn = lfn * (lax.broadcasted_iota(jnp.int32, (1, 8, 1), 1) < H_FOX)
        o = _sample_attn(page_table, q.reshape(n_seq, dec_q, MIX_W), k.reshape(n_seq, dec_q, MIX_W),
                         v.reshape(n_seq, dec_q, MIX_W), lfn, ck, cv, clf, lam_p, gsub,
                         layer=l, lam_init=lam_init)
        xs = _mlp(_out_proj(o.reshape(n_seq * dec_q, MIX_W), wo, xs), g2, wu, wd)
        outs["ks"].append(k.reshape(n_seq, dec_q, N_HEADS, HEAD_DIM))
        outs["vs"].append(v.reshape(n_seq, dec_q, N_HEADS, HEAD_DIM))
        outs["fs"].append(lf[:, :H_FOX].reshape(n_seq, dec_q, H_FOX))

    return (xp.reshape(batch, t, d), xs.reshape(n_seq, dec_q, d),
            jnp.stack(outs["kp"]), jnp.stack(outs["vp"]), jnp.stack(outs["fp"]),
            jnp.stack(outs["ks"]), jnp.stack(outs["vs"]), jnp.stack(outs["fs"]))
```

```python
import functools
import math

import jax
import jax.numpy as jnp
from jax import lax
from jax.experimental import pallas as pl
from jax.experimental.pallas import tpu as pltpu

HEAD_DIM = 128
H_SB = 4
H_FOX = 6
H_DIFF = 6
N_HEADS = H_SB + H_FOX + H_DIFF
MIX_W = N_HEADS * HEAD_DIM
DIFF_QK = HEAD_DIM // 2
EPS = 1e-6
NEG = -1e30
ALIBI_RATE = 8.0 / H_DIFF
LOG2E = math.log2(math.e)
LANES = 128
SB_CUTOFF = 152.0

DEC_Q = 8
SB_ROWS = H_SB * DEC_Q
FOX_ROWS = H_FOX * DEC_Q
DIFF_ROWS = H_DIFF * 2 * DEC_Q
FOX_ROW0 = SB_ROWS
DIFF_ROW0 = SB_ROWS + FOX_ROWS
ALL_ROWS = SB_ROWS + FOX_ROWS + DIFF_ROWS
HALF_HEADS = N_HEADS // 2

VMEM_LIMIT = 56 * 1024 * 1024

f32 = jnp.float32
bf16 = jnp.bfloat16


def _dot(a, b):
    return jnp.dot(a, b, preferred_element_type=f32)


def _dot_nt(a, b):
    return lax.dot_general(a, b, (((1,), (1,)), ((), ())), preferred_element_type=f32)


def _dot_exact(a, b):
    return jnp.dot(a, b, preferred_element_type=f32, precision=lax.Precision.HIGHEST)


def _log_sigmoid(x):
    return jnp.minimum(x, 0.0) - jnp.log1p(jnp.exp(-jnp.abs(x)))


def _split_bf16(x):
    hi = x.astype(bf16)
    lo = (x - hi.astype(f32)).astype(bf16)
    return hi, lo


def _lam(lam_ref, lam_init):
    lv = lam_ref[...]
    d1 = jnp.sum(lv[0:1, :] * lv[1:2, :], axis=1, keepdims=True)
    d2 = jnp.sum(lv[2:3, :] * lv[3:4, :], axis=1, keepdims=True)
    return jnp.exp(d1) - jnp.exp(d2) + lam_init


def _diff_finish(o1, o2, lam, gsub, lam_init):
    o = o1 - lam * o2
    ms = jnp.mean(o * o, axis=-1, keepdims=True)
    return o * lax.rsqrt(ms + EPS) * gsub * (1.0 - lam_init)


def _suffix_matrix():
    r = lax.broadcasted_iota(jnp.int32, (2 * LANES, 2 * LANES), 0) & (LANES - 1)
    c = lax.broadcasted_iota(jnp.int32, (2 * LANES, 2 * LANES), 1)
    return jnp.where((c >= LANES) | (r > c), 1.0, 0.0).astype(bf16)


def _sb_weights(z, strict, carry, suffix):
    c = jnp.maximum(z, 0.0) + jnp.log2(1.0 + jnp.exp2(-jnp.abs(z)))
    if strict is not None:
        c = jnp.where(strict, c, 0.0)
    ws = []
    for j in reversed(range(z.shape[1] // LANES)):
        cj = c[:, j * LANES:(j + 1) * LANES]
        hi, lo = _split_bf16(cj)
        t = _dot(jnp.concatenate([hi, lo], axis=1), suffix)
        ws.append(jnp.exp2(z[:, j * LANES:(j + 1) * LANES] - cj - t[:, :LANES] - carry))
        carry = carry + t[:, LANES:]
    w = jnp.concatenate(ws[::-1], axis=1) if len(ws) > 1 else ws[0]
    if strict is not None:
        w = jnp.where(strict, w, 0.0)
    return w, carry


def _norm_heads(z, g, group):
    parts = []
    for hh in range(z.shape[1] // HEAD_DIM):
        zz = z[:, hh * HEAD_DIM:(hh + 1) * HEAD_DIM]
        z2 = zz * zz
        if group == HEAD_DIM:
            ms = jnp.mean(z2, axis=-1, keepdims=True)
        else:
            lo = lax.broadcasted_iota(jnp.int32, (1, HEAD_DIM), 1) < DIFF_QK
            s_lo = jnp.sum(jnp.where(lo, z2, 0.0), axis=-1, keepdims=True)
            s_hi = jnp.sum(jnp.where(lo, 0.0, z2), axis=-1, keepdims=True)
            ms = jnp.where(lo, s_lo, s_hi) * (1.0 / DIFF_QK)
        parts.append(zz * lax.rsqrt(ms + EPS) * g[:, hh * HEAD_DIM:(hh + 1) * HEAD_DIM])
    return jnp.concatenate(parts, axis=1)


def _in_proj_kernel(x_ref, g1_ref, wq_ref, wk_ref, wv_ref, wf_ref, bf_ref, gq_ref, gk_ref,
                    q_ref, k_ref, v_ref, lf_ref, *rest, emit_bf16_kv):
    if emit_bf16_kv:
        kb_ref, vb_ref, h_scr = rest
    else:
        (h_scr,) = rest
    j = pl.program_id(1)

    @pl.when(j == 0)
    def _():
        x = x_ref[...]
        ms = jnp.mean(x * x, axis=-1, keepdims=True)
        h_scr[...] = (x * lax.rsqrt(ms + EPS) * g1_ref[...]).astype(bf16)
        lf_ref[...] = _log_sigmoid(_dot(h_scr[...], wf_ref[...]) + bf_ref[...])

    h = h_scr[...]
    zq = _dot(h, wq_ref[...])
    zk = _dot(h, wk_ref[...])
    zv = _dot(h, wv_ref[...])
    v_ref[...] = zv
    if emit_bf16_kv:
        vb_ref[...] = zv.astype(bf16)

    def store(q, k):
        q_ref[...] = q.astype(q_ref.dtype)
        k_ref[...] = k
        if emit_bf16_kv:
            kb_ref[...] = k.astype(bf16)

    @pl.when(j < H_SB // 2)
    def _():
        store(zq * (LOG2E * HEAD_DIM ** -0.5), zk)

    @pl.when((j >= H_SB // 2) & (j < (H_SB + H_FOX) // 2))
    def _():
        store(_norm_heads(zq, gq_ref[...], HEAD_DIM) * (LOG2E * HEAD_DIM ** -0.5),
              _norm_heads(zk, gk_ref[...], HEAD_DIM))

    @pl.when(j >= (H_SB + H_FOX) // 2)
    def _():
        store(_norm_heads(zq, gq_ref[...], DIFF_QK) * (LOG2E * DIFF_QK ** -0.5),
              _norm_heads(zk, gk_ref[...], DIFF_QK))


def _in_proj(x, g1, wq, wk, wv, wf, bfp, gq, gk, *, q_dtype, emit_bf16_kv):
    m, d = x.shape
    tm = min(1024, m)
    tn = 2 * HEAD_DIM
    row = lambda i, j: (i, 0)
    col = lambda i, j: (0, j)
    tile = lambda i, j: (i, j)
    const = lambda i, j: (0, 0)
    out_shape = [jax.ShapeDtypeStruct((m, MIX_W), q_dtype),
                 jax.ShapeDtypeStruct((m, MIX_W), f32),
                 jax.ShapeDtypeStruct((m, MIX_W), f32),
                 jax.ShapeDtypeStruct((m, HEAD_DIM), f32)]
    out_specs = [pl.BlockSpec((tm, tn), tile), pl.BlockSpec((tm, tn), tile), pl.BlockSpec((tm, tn), tile),
                 pl.BlockSpec((tm, HEAD_DIM), row)]
    if emit_bf16_kv:
        out_shape += [jax.ShapeDtypeStruct((m, MIX_W), bf16)] * 2
        out_specs += [pl.BlockSpec((tm, tn), tile)] * 2
    return pl.pallas_call(
        functools.partial(_in_proj_kernel, emit_bf16_kv=emit_bf16_kv),
        grid=(m // tm, MIX_W // tn),
        in_specs=[pl.BlockSpec((tm, d), row), pl.BlockSpec((1, d), const),
                  pl.BlockSpec((d, tn), col), pl.BlockSpec((d, tn), col), pl.BlockSpec((d, tn), col),
                  pl.BlockSpec((d, HEAD_DIM), const), pl.BlockSpec((1, HEAD_DIM), const),
                  pl.BlockSpec((1, tn), col), pl.BlockSpec((1, tn), col)],
        out_specs=out_specs,
        out_shape=out_shape,
        scratch_shapes=[pltpu.VMEM((tm, d), bf16)],
        compiler_params=pltpu.CompilerParams(dimension_semantics=("arbitrary", "arbitrary"),
                                             vmem_limit_bytes=VMEM_LIMIT),
        name="in_proj",
    )(x, g1, wq, wk, wv, wf, bfp, gq, gk)


def _cumsum_kernel(lf_ref, ft_ref, carry_scr):
    @pl.when(pl.program_id(1) == 0)
    def _():
        carry_scr[...] = jnp.zeros_like(carry_scr)

    tc = lf_ref.shape[0]
    lf_t = lf_ref[...].T[:8]
    r = lax.broadcasted_iota(jnp.int32, (tc, tc), 0)
    c = lax.broadcasted_iota(jnp.int32, (tc, tc), 1)
    cs = _dot_exact(lf_t, (r <= c).astype(f32)) + carry_scr[:, 0:1]
    ft_ref[...] = cs
    carry_scr[...] = jnp.broadcast_to(cs[:, tc - 1:tc], carry_scr.shape)


def _cumsum_t(lf, batch, t):
    tc = min(512, t)
    nblk = t // tc
    return pl.pallas_call(
        _cumsum_kernel,
        grid=(batch, nblk),
        in_specs=[pl.BlockSpec((tc, HEAD_DIM), lambda b, i: (b * nblk + i, 0))],
        out_specs=pl.BlockSpec((None, 8, tc), lambda b, i: (b, 0, i)),
        out_shape=jax.ShapeDtypeStruct((batch, 8, t), f32),
        scratch_shapes=[pltpu.VMEM((8, HEAD_DIM), f32)],
        compiler_params=pltpu.CompilerParams(dimension_semantics=("arbitrary", "arbitrary")),
        name="logf_cumsum",
    )(lf)


def _softmax_block(s, v1, m, accl):
    m_new = jnp.maximum(m, jnp.max(s, axis=1, keepdims=True))
    alpha = jnp.exp2(m - m_new)
    p = jnp.exp2(s - m_new)
    return m_new, alpha * accl + _dot(p.astype(bf16), v1)


def _prompt_attn_kernel(q_ref, k_ref, v_ref, ft_ref, lam_ref, gsub_ref, o_ref, *, tq, lam_init):
    h = pl.program_id(1)
    qi = pl.program_id(2)
    q = q_ref[...]
    row = lax.broadcasted_iota(jnp.int32, (tq, tq), 0)
    col = lax.broadcasted_iota(jnp.int32, (tq, tq), 1)

    def kv(kb, with_ones):
        off = pl.multiple_of(kb * tq, tq)
        v = v_ref[pl.ds(off, tq), :]
        if with_ones:
            v = jnp.concatenate([v, jnp.ones((tq, HEAD_DIM), bf16)], axis=1)
        return k_ref[pl.ds(off, tq), :], v, off

    @pl.when(h < H_SB)
    def _():
        suffix = _suffix_matrix()

        def block(kb, carry, acc, masked):
            k, v, _ = kv(kb, False)
            w, carry = _sb_weights(_dot_nt(q, k), (col < row) if masked else None, carry, suffix)
            return carry, acc + _dot(w.astype(bf16), v)

        carry, acc = block(qi, jnp.zeros((tq, LANES), f32), jnp.zeros((tq, HEAD_DIM), f32), True)

        def cond(st):
            return (st[0] >= 0) & (jnp.min(st[1]) < SB_CUTOFF)

        def body(st):
            carry, acc = block(st[0], st[1], st[2], False)
            return st[0] - 1, carry, acc

        _, _, acc = lax.while_loop(cond, body, (qi - 1, carry, acc))
        o_ref[...] = acc.astype(o_ref.dtype)

    @pl.when((h >= H_SB) & (h < H_SB + H_FOX))
    def _():
        fx = h - H_SB

        def block(kb, m, accl, masked):
            k, v1, off = kv(kb, True)
            s = _dot_nt(q, k) - LOG2E * ft_ref[pl.ds(fx, 1), pl.ds(off, tq)]
            if masked:
                s = jnp.where(col <= row, s, NEG)
            return _softmax_block(s, v1, m, accl)

        state = (jnp.full((tq, 1), NEG, f32), jnp.zeros((tq, 2 * HEAD_DIM), f32))
        state = lax.fori_loop(0, qi, lambda kb, s: block(kb, *s, False), state)
        _, accl = block(qi, *state, True)
        o_ref[...] = (accl[:, :HEAD_DIM] / accl[:, HEAD_DIM:]).astype(o_ref.dtype)

    @pl.when(h >= H_SB + H_FOX)
    def _():
        hd = h - (H_SB + H_FOX)
        slope = LOG2E * jnp.exp2(jnp.full((1, 1), -ALIBI_RATE, f32) * (hd + 1).astype(f32))
        lane = lax.broadcasted_iota(jnp.int32, (tq, HEAD_DIM), 1)
        zero = jnp.zeros_like(q)
        q2 = jnp.concatenate([jnp.where(lane < DIFF_QK, q, zero), jnp.where(lane < DIFF_QK, zero, q)], axis=0)
        row2 = lax.broadcasted_iota(jnp.int32, (2 * tq, tq), 0) & (tq - 1)
        col2 = lax.broadcasted_iota(jnp.int32, (2 * tq, tq), 1)
        kidx = lax.broadcasted_iota(jnp.int32, (1, tq), 1).astype(f32)

        def block(kb, m, accl, masked):
            k, v1, _ = kv(kb, True)
            kpos = kidx + ((kb - qi) * tq).astype(f32)
            s = _dot_nt(q2, k) + slope * kpos
            if masked:
                s = jnp.where(col2 <= row2, s, NEG)
            return _softmax_block(s, v1, m, accl)

        state = (jnp.full((2 * tq, 1), NEG, f32), jnp.zeros((2 * tq, 2 * HEAD_DIM), f32))
        state = lax.fori_loop(0, qi, lambda kb, s: block(kb, *s, False), state)
        _, accl = block(qi, *state, True)
        o = accl[:, :HEAD_DIM] / accl[:, HEAD_DIM:]
        out = _diff_finish(o[:tq], o[tq:], _lam(lam_ref, lam_init), gsub_ref[...], lam_init)
        o_ref[...] = out.astype(o_ref.dtype)


def _prompt_attn(q, k, v, ft, lam_p, gsub, *, batch, t, lam_init):
    tq = min(512, t)
    nq = t // tq
    full = lambda b, h, i: (0, 0)
    return pl.pallas_call(
        functools.partial(_prompt_attn_kernel, tq=tq, lam_init=lam_init),
        grid=(batch, N_HEADS, nq),
        in_specs=[pl.BlockSpec((tq, HEAD_DIM), lambda b, h, i: (b * nq + i, h)),
                  pl.BlockSpec((t, HEAD_DIM), lambda b, h, i: (b, h)),
                  pl.BlockSpec((t, HEAD_DIM), lambda b, h, i: (b, h)),
                  pl.BlockSpec((None, 8, t), lambda b, h, i: (b, 0, 0)),
                  pl.BlockSpec(lam_p.shape, full), pl.BlockSpec(gsub.shape, full)],
        out_specs=pl.BlockSpec((tq, HEAD_DIM), lambda b, h, i: (b * nq + i, h)),
        out_shape=jax.ShapeDtypeStruct((batch * t, MIX_W), bf16),
        compiler_params=pltpu.CompilerParams(dimension_semantics=("arbitrary", "arbitrary", "arbitrary"),
                                             vmem_limit_bytes=VMEM_LIMIT),
        name="prompt_attn",
    )(q, k, v, ft, lam_p, gsub)


def _head_rows(ref, r):
    keys = ref.shape[0]
    return ref.reshape(keys * 8, HEAD_DIM)[pl.ds(r, keys, stride=8), :]


def _chunk_rows(c):
    h0 = 2 * c
    if h0 < H_SB + H_FOX:
        return h0 * DEC_Q, DEC_Q
    return DIFF_ROW0 + (h0 - H_SB - H_FOX) * 2 * DEC_Q, 2 * DEC_Q


def _sample_attn_kernel(pt_ref, q_ref, kn_ref, vn_ref, lfn_ref, *rest, pages_per_step, n_groups, past_len,
                        lam_init):
    pp = pages_per_step
    k_refs = rest[:2 * pp]
    v_refs = rest[2 * pp:4 * pp]
    lf_refs = rest[4 * pp:5 * pp]
    lam_ref, gsub_ref, o_ref, qbd_scr, acc_scr, m_scr, l_scr, hc_scr = rest[5 * pp:]
    g = pl.program_id(1)
    page = lf_refs[0].shape[1]

    r_c = lax.broadcasted_iota(jnp.int32, (page, page), 0)
    c_c = lax.broadcasted_iota(jnp.int32, (page, page), 1)
    after_f32 = jnp.where(r_c > c_c, 1.0, 0.0)
    suffix = _suffix_matrix()
    head = lax.broadcasted_iota(jnp.int32, (DIFF_ROWS, 1), 0) >> 4
    slope = LOG2E * jnp.exp2(-ALIBI_RATE * (head + 1).astype(f32))

    def qidx(rows):
        return lax.broadcasted_iota(jnp.int32, (rows, 1), 0) & (DEC_Q - 1)

    def process(load_k, load_v, lfs, kpos0s, is_new):
        n_p = len(lfs)
        n = n_p * page
        col = lax.broadcasted_iota(jnp.int32, (1, n), 1)

        def chunk(load, c):
            rows = [jnp.concatenate([load(p, 2 * c), load(p, 2 * c + 1)], axis=1) for p in range(n_p)]
            return (jnp.concatenate(rows, axis=0) if n_p > 1 else rows[0]).astype(bf16)

        scores = []
        for c in range(HALF_HEADS):
            r0, per = _chunk_rows(c)
            scores.append(_dot_nt(qbd_scr[r0:r0 + 2 * per, :], chunk(load_k, c)))
        s_all = jnp.concatenate(scores, axis=0)

        sb_ok = (col < qidx(SB_ROWS)) if is_new else None
        w, carry = _sb_weights(s_all[:SB_ROWS], sb_ok, l_scr[:SB_ROWS], suffix)
        l_scr[:SB_ROWS] = carry

        later = [None] * n_p
        hc = hc_scr[:, 0:1]
        for p in reversed(range(n_p)):
            later[p] = _dot_exact(lfs[p], after_f32) + hc
            hc = hc + jnp.sum(lfs[p], axis=1, keepdims=True)
        hc_scr[...] = jnp.broadcast_to(hc, hc_scr.shape)
        later = jnp.concatenate(later, axis=1) if n_p > 1 else later[0]
        later_rows = jnp.concatenate(
            [jnp.broadcast_to(later[f:f + 1], (DEC_Q, n)) for f in range(H_FOX)], axis=0)
        s_fx = s_all[FOX_ROW0:DIFF_ROW0] + LOG2E * later_rows

        kpos = [lax.broadcasted_iota(jnp.int32, (1, page), 1).astype(f32) + kpos0s[p] for p in range(n_p)]
        kpos = jnp.concatenate(kpos, axis=1) if n_p > 1 else kpos[0]
        s_df = s_all[DIFF_ROW0:] + slope * kpos

        s = jnp.concatenate([s_fx, s_df], axis=0)
        if is_new:
            s = jnp.where(col <= qidx(FOX_ROWS + DIFF_ROWS), s, NEG)
        m_old = m_scr[FOX_ROW0:]
        m_new = jnp.maximum(m_old, jnp.max(s, axis=1, keepdims=True))
        alpha = jnp.exp2(m_old - m_new)
        p_sm = jnp.exp2(s - m_new)
        m_scr[FOX_ROW0:] = m_new
        l_scr[FOX_ROW0:] = alpha * l_scr[FOX_ROW0:] + jnp.sum(p_sm, axis=1, keepdims=True)

        pw = jnp.concatenate([w, p_sm], axis=0).astype(bf16)
        pieces = []
        for c in range(HALF_HEADS):
            r0, per = _chunk_rows(c)
            res = _dot(pw[r0:r0 + 2 * per], chunk(load_v, c))
            pieces.append(res[:per, :HEAD_DIM])
            pieces.append(res[per:, HEAD_DIM:])
        pv = jnp.concatenate(pieces, axis=0)
        scale = jnp.concatenate([jnp.ones((SB_ROWS, 1), f32), alpha], axis=0)
        acc_scr[...] = scale * acc_scr[...] + pv

    @pl.when(g == 0)
    def _():
        q = q_ref[...]
        lane = lax.broadcasted_iota(jnp.int32, (DEC_Q, HEAD_DIM), 1)
        zero = jnp.zeros((DEC_Q, HEAD_DIM), f32)
        rows = []
        for hh in range(N_HEADS):
            tile = q[:, hh * HEAD_DIM:(hh + 1) * HEAD_DIM]
            if hh < H_SB + H_FOX:
                tiles = [tile]
            else:
                tiles = [jnp.where(lane < DIFF_QK, tile, 0.0), jnp.where(lane < DIFF_QK, 0.0, tile)]
            for tl in tiles:
                rows.append(jnp.concatenate([tl, zero] if hh % 2 == 0 else [zero, tl], axis=1))
        qbd_scr[...] = jnp.concatenate(rows, axis=0).astype(bf16)
        acc_scr[...] = jnp.zeros_like(acc_scr)
        l_scr[...] = jnp.zeros_like(l_scr)
        hc_scr[...] = jnp.zeros_like(hc_scr)
        m_scr[...] = jnp.full(m_scr.shape, NEG, f32)
        pad = jnp.zeros((page - DEC_Q, HEAD_DIM), f32)
        process(lambda p, hh: jnp.concatenate([kn_ref[:, hh * HEAD_DIM:(hh + 1) * HEAD_DIM], pad], axis=0),
                lambda p, hh: jnp.concatenate([vn_ref[:, hh * HEAD_DIM:(hh + 1) * HEAD_DIM], pad], axis=0),
                [lfn_ref[...]], [0.0], True)

    def load_from(refs):
        return lambda p, hh: _head_rows(refs[2 * p + hh // 8], hh % 8)

    first_page = (n_groups - 1 - g) * pp
    process(load_from(k_refs), load_from(v_refs), [lf_refs[p][...] for p in range(pp)],
            [((first_page + p) * page - past_len).astype(f32) for p in range(pp)], False)

    @pl.when(g == n_groups - 1)
    def _():
        acc = acc_scr[...]
        l = l_scr[...]
        lam = _lam(lam_ref, lam_init)
        gsub = gsub_ref[...]
        outs = []
        for hh in range(N_HEADS):
            if hh < H_SB:
                outs.append(acc[hh * DEC_Q:(hh + 1) * DEC_Q])
            elif hh < H_SB + H_FOX:
                r0 = hh * DEC_Q
                outs.append(acc[r0:r0 + DEC_Q] / l[r0:r0 + DEC_Q])
            else:
                r0 = DIFF_ROW0 + (hh - H_SB - H_FOX) * 2 * DEC_Q
                o1 = acc[r0:r0 + DEC_Q] / l[r0:r0 + DEC_Q]
                o2 = acc[r0 + DEC_Q:r0 + 2 * DEC_Q] / l[r0 + DEC_Q:r0 + 2 * DEC_Q]
                outs.append(_diff_finish(o1, o2, lam, gsub, lam_init))
        o_ref[...] = jnp.concatenate(outs, axis=1)


def _sample_attn(page_table, q, kn, vn, lfn, cache_k, cache_v, cache_lf, lam_p, gsub, *, layer, lam_init):
    n_seq, n_pages = page_table.shape
    page = cache_lf.shape[3]
    pp = 4 if n_pages % 4 == 0 else (2 if n_pages % 2 == 0 else 1)
    n_groups = n_pages // pp
    assert q.shape[1] == DEC_Q and page == LANES

    def page_index(b, g, pt, i):
        return pt[b, (n_groups - 1 - g) * pp + i]

    def half_map(i, half):
        return lambda b, g, pt: (layer, page_index(b, g, pt, i), 0, half, 0, 0)

    def lf_map(i):
        return lambda b, g, pt: (layer, page_index(b, g, pt, i), 0, 0)

    seq = lambda b, g, pt: (b, 0, 0)
    full = lambda b, g, pt: (0, 0)
    half_block = (None, None, page, None, 8, HEAD_DIM)
    in_specs = [pl.BlockSpec((None, DEC_Q, MIX_W), seq), pl.BlockSpec((None, DEC_Q, MIX_W), seq),
                pl.BlockSpec((None, DEC_Q, MIX_W), seq), pl.BlockSpec((None, 8, page), seq)]
    for _ in range(2):
        in_specs += [pl.BlockSpec(half_block, half_map(i, half)) for i in range(pp) for half in range(2)]
    in_specs += [pl.BlockSpec((None, None, 8, page), lf_map(i)) for i in range(pp)]
    in_specs += [pl.BlockSpec(lam_p.shape, full), pl.BlockSpec(gsub.shape, full)]
    grid_spec = pltpu.PrefetchScalarGridSpec(
        num_scalar_prefetch=1,
        grid=(n_seq, n_groups),
        in_specs=in_specs,
        out_specs=pl.BlockSpec((None, DEC_Q, MIX_W), seq),
        scratch_shapes=[pltpu.VMEM((ALL_ROWS, 2 * HEAD_DIM), bf16), pltpu.VMEM((ALL_ROWS, HEAD_DIM), f32),
                        pltpu.VMEM((ALL_ROWS, 1), f32), pltpu.VMEM((ALL_ROWS, LANES), f32),
                        pltpu.VMEM((8, LANES), f32)],
    )
    return pl.pallas_call(
        functools.partial(_sample_attn_kernel, pages_per_step=pp, n_groups=n_groups,
                          past_len=n_pages * page, lam_init=lam_init),
        grid_spec=grid_spec,
        out_shape=jax.ShapeDtypeStruct((n_seq, DEC_Q, MIX_W), f32),
        compiler_params=pltpu.CompilerParams(dimension_semantics=("arbitrary", "arbitrary"),
                                             vmem_limit_bytes=VMEM_LIMIT),
        name="sample_attn",
    )(page_table, q, kn, vn, lfn, *([cache_k] * (2 * pp)), *([cache_v] * (2 * pp)), *([cache_lf] * pp),
      lam_p, gsub)


def _out_proj_kernel(o_ref, w_ref, x_ref, y_ref):
    y_ref[...] = x_ref[...] + _dot(o_ref[...].astype(bf16), w_ref[...])


def _out_proj(o, w, x):
    m, d = x.shape
    tm = min(512, m)
    tn = min(1024, d)
    return pl.pallas_call(
        _out_proj_kernel,
        grid=(m // tm, d // tn),
        in_specs=[pl.BlockSpec((tm, o.shape[1]), lambda i, j: (i, 0)),
                  pl.BlockSpec((w.shape[0], tn), lambda i, j: (0, j)),
                  pl.BlockSpec((tm, tn), lambda i, j: (i, j))],
        out_specs=pl.BlockSpec((tm, tn), lambda i, j: (i, j)),
        out_shape=jax.ShapeDtypeStruct((m, d), f32),
        compiler_params=pltpu.CompilerParams(dimension_semantics=("arbitrary", "arbitrary"),
                                             vmem_limit_bytes=VMEM_LIMIT),
        name="out_proj",
    )(o, w, x)


def _mlp_kernel(x_ref, g2_ref, wu_ref, wd_ref, y_ref, h_scr):
    j = pl.program_id(1)

    @pl.when(j == 0)
    def _():
        x = x_ref[...]
        ms = jnp.mean(x * x, axis=-1, keepdims=True)
        h_scr[...] = (x * lax.rsqrt(ms + EPS) * g2_ref[...]).astype(bf16)
        y_ref[...] = x

    u = jnp.maximum(_dot(h_scr[...], wu_ref[...]), 0.0)
    y_ref[...] += _dot((u * u).astype(bf16), wd_ref[...])


def _mlp(x, g2, wu, wd):
    m, d = x.shape
    ff = wu.shape[1]
    tm = min(512, m)
    tf = min(1024, ff)
    return pl.pallas_call(
        _mlp_kernel,
        grid=(m // tm, ff // tf),
        in_specs=[pl.BlockSpec((tm, d), lambda i, j: (i, 0)), pl.BlockSpec((1, d), lambda i, j: (0, 0)),
                  pl.BlockSpec((d, tf), lambda i, j: (0, j)), pl.BlockSpec((tf, d), lambda i, j: (j, 0))],
        out_specs=pl.BlockSpec((tm, d), lambda i, j: (i, 0)),
        out_shape=jax.ShapeDtypeStruct((m, d), f32),
        scratch_shapes=[pltpu.VMEM((tm, d), bf16)],
        compiler_params=pltpu.CompilerParams(dimension_semantics=("arbitrary", "arbitrary"),
                                             vmem_limit_bytes=VMEM_LIMIT),
        name="mlp",
    )(x, g2, wu, wd)


def kernel(x_prompt, x_sample, cache_k, cache_v, cache_logf, page_table, ln1_g, w_in, b_f, g_q_fox, g_k_fox,
           g_q_diff, g_k_diff, diff_lambda, g_sub_diff, w_out, ln2_g, w_up, w_down):
    batch, t, d = x_prompt.shape
    n_seq, dec_q, _ = x_sample.shape
    depth, n_phys, page = cache_k.shape[:3]

    xp = x_prompt.reshape(batch * t, d)
    xs = x_sample.reshape(n_seq * dec_q, d)
    ck = cache_k.reshape(depth, n_phys, page, 2, 8, HEAD_DIM)
    cv = cache_v.reshape(depth, n_phys, page, 2, 8, HEAD_DIM)
    clf = jnp.pad(cache_logf.astype(f32).transpose(0, 1, 3, 2), ((0, 0), (0, 0), (0, 8 - H_FOX), (0, 0)))

    ones_sb = jnp.ones((H_SB * HEAD_DIM,), f32)
    outs = {name: [] for name in ("kp", "vp", "fp", "ks", "vs", "fs")}
    for l in range(depth):
        lam_init = 0.8 - 0.6 * math.exp(-0.3 * l)
        wq = w_in[l, :, :MIX_W].astype(bf16)
        wk = w_in[l, :, MIX_W:2 * MIX_W].astype(bf16)
        wv = w_in[l, :, 2 * MIX_W:3 * MIX_W].astype(bf16)
        wf = jnp.pad(w_in[l, :, 3 * MIX_W:], ((0, 0), (0, HEAD_DIM - H_FOX))).astype(bf16)
        bfp = jnp.pad(b_f[l].astype(f32), (0, HEAD_DIM - H_FOX)).reshape(1, HEAD_DIM)
        gq = jnp.concatenate([ones_sb, jnp.tile(g_q_fox[l], H_FOX), jnp.tile(g_q_diff[l].reshape(-1), H_DIFF)])
        gk = jnp.concatenate([ones_sb, jnp.tile(g_k_fox[l], H_FOX), jnp.tile(g_k_diff[l].reshape(-1), H_DIFF)])
        gq = gq.astype(f32).reshape(1, MIX_W)
        gk = gk.astype(f32).reshape(1, MIX_W)
        g1 = ln1_g[l].astype(f32).reshape(1, d)
        g2 = ln2_g[l].astype(f32).reshape(1, d)
        lam_p = diff_lambda[l].astype(f32)
        gsub = g_sub_diff[l].astype(f32).reshape(1, HEAD_DIM)
        wo = w_out[l].astype(bf16)
        wu = w_up[l].astype(bf16)
        wd = w_down[l].astype(bf16)

        q, k, v, lf, kb, vb = _in_proj(xp, g1, wq, wk, wv, wf, bfp, gq, gk, q_dtype=bf16, emit_bf16_kv=True)
        ft = _cumsum_t(lf, batch, t)
        o = _prompt_attn(q, kb, vb, ft, lam_p, gsub, batch=batch, t=t, lam_init=lam_init)
        xp = _mlp(_out_proj(o, wo, xp), g2, wu, wd)
        outs["kp"].append(k.reshape(batch, t, N_HEADS, HEAD_DIM))
        outs["vp"].append(v.reshape(batch, t, N_HEADS, HEAD_DIM))
        outs["fp"].append(lf[:, :H_FOX].reshape(batch, t, H_FOX))

        q, k, v, lf = _in_proj(xs, g1, wq, wk, wv, wf, bfp, gq, gk, q_dtype=f32, emit_bf16_kv=False)
        lfn = lf.reshape(n_seq, dec_q, HEAD_DIM)[:, :, :8].transpose(0, 2, 1)
        lfn = jnp.pad(lfn, ((0, 0), (0, 0), (0, page - dec_q)))
        o = _sample_attn(page_table, q.reshape(n_seq, dec_q, MIX_W), k.reshape(n_seq, dec_q, MIX_W),
                         v.reshape(n_seq, dec_q, MIX_W), lfn, ck, cv, clf, lam_p, gsub,
                         layer=l, lam_init=lam_init)
        xs = _mlp(_out_proj(o.reshape(n_seq * dec_q, MIX_W), wo, xs), g2, wu, wd)
        outs["ks"].append(k.reshape(n_seq, dec_q, N_HEADS, HEAD_DIM))
        outs["vs"].append(v.reshape(n_seq, dec_q, N_HEADS, HEAD_DIM))
        outs["fs"].append(lf[:, :H_FOX].reshape(n_seq, dec_q, H_FOX))

    return (xp.reshape(batch, t, d), xs.reshape(n_seq, dec_q, d),
            jnp.stack(outs["kp"]), jnp.stack(outs["vp"]), jnp.stack(outs["fp"]),
            jnp.stack(outs["ks"]), jnp.stack(outs["vs"]), jnp.stack(outs["fs"]))
```

```python
import functools
import math

import jax
import jax.numpy as jnp
from jax import lax
from jax.experimental import pallas as pl
from jax.experimental.pallas import tpu as pltpu

HEAD_DIM = 128
H_SB = 4
H_FOX = 6
H_DIFF = 6
N_HEADS = H_SB + H_FOX + H_DIFF
MIX_W = N_HEADS * HEAD_DIM
DIFF_QK = HEAD_DIM // 2
EPS = 1e-6
NEG = -1e30
ALIBI_RATE = 8.0 / H_DIFF
LOG2E = math.log2(math.e)
LANES = 128
SB_CUTOFF = 152.0

DEC_Q = 8
SB_ROWS = H_SB * DEC_Q
FOX_ROWS = H_FOX * DEC_Q
DIFF_ROWS = H_DIFF * 2 * DEC_Q
FOX_ROW0 = SB_ROWS
DIFF_ROW0 = SB_ROWS + FOX_ROWS
ALL_ROWS = SB_ROWS + FOX_ROWS + DIFF_ROWS
HALF_HEADS = N_HEADS // 2

VMEM_LIMIT = 56 * 1024 * 1024
MAX_PAGES_PER_STEP = 8

f32 = jnp.float32
bf16 = jnp.bfloat16


def _dot(a, b):
    return jnp.dot(a, b, preferred_element_type=f32)


def _dot_nt(a, b):
    return lax.dot_general(a, b, (((1,), (1,)), ((), ())), preferred_element_type=f32)


def _dot_exact(a, b):
    return jnp.dot(a, b, preferred_element_type=f32, precision=lax.Precision.HIGHEST)


def _log_sigmoid(x):
    return jnp.minimum(x, 0.0) - jnp.log1p(jnp.exp(-jnp.abs(x)))


def _split_bf16(x):
    hi = x.astype(bf16)
    lo = (x - hi.astype(f32)).astype(bf16)
    return hi, lo


def _lam(lam_ref, lam_init):
    lv = lam_ref[...]
    d1 = jnp.sum(lv[0:1, :] * lv[1:2, :], axis=1, keepdims=True)
    d2 = jnp.sum(lv[2:3, :] * lv[3:4, :], axis=1, keepdims=True)
    return jnp.exp(d1) - jnp.exp(d2) + lam_init


def _diff_finish(o1, o2, lam, gsub, lam_init):
    o = o1 - lam * o2
    ms = jnp.mean(o * o, axis=-1, keepdims=True)
    return o * lax.rsqrt(ms + EPS) * gsub * (1.0 - lam_init)


def _suffix_matrix():
    r = lax.broadcasted_iota(jnp.int32, (2 * LANES, 2 * LANES), 0) & (LANES - 1)
    c = lax.broadcasted_iota(jnp.int32, (2 * LANES, 2 * LANES), 1)
    return jnp.where((c >= LANES) | (r > c), 1.0, 0.0).astype(bf16)


def _sb_weights(z, strict, carry, suffix):
    c = jnp.maximum(z, 0.0) + jnp.log2(1.0 + jnp.exp2(-jnp.abs(z)))
    if strict is not None:
        c = jnp.where(strict, c, 0.0)
    ws = []
    for j in reversed(range(z.shape[1] // LANES)):
        cj = c[:, j * LANES:(j + 1) * LANES]
        hi, lo = _split_bf16(cj)
        t = _dot(jnp.concatenate([hi, lo], axis=1), suffix)
        ws.append(jnp.exp2(z[:, j * LANES:(j + 1) * LANES] - cj - t[:, :LANES] - carry))
        carry = carry + t[:, LANES:]
    w = jnp.concatenate(ws[::-1], axis=1) if len(ws) > 1 else ws[0]
    if strict is not None:
        w = jnp.where(strict, w, 0.0)
    return w, carry


def _norm_heads(z, g, group):
    parts = []
    for hh in range(z.shape[1] // HEAD_DIM):
        zz = z[:, hh * HEAD_DIM:(hh + 1) * HEAD_DIM]
        z2 = zz * zz
        if group == HEAD_DIM:
            ms = jnp.mean(z2, axis=-1, keepdims=True)
        else:
            lo = lax.broadcasted_iota(jnp.int32, (1, HEAD_DIM), 1) < DIFF_QK
            s_lo = jnp.sum(jnp.where(lo, z2, 0.0), axis=-1, keepdims=True)
            s_hi = jnp.sum(jnp.where(lo, 0.0, z2), axis=-1, keepdims=True)
            ms = jnp.where(lo, s_lo, s_hi) * (1.0 / DIFF_QK)
        parts.append(zz * lax.rsqrt(ms + EPS) * g[:, hh * HEAD_DIM:(hh + 1) * HEAD_DIM])
    return jnp.concatenate(parts, axis=1)


def _in_proj_kernel(x_ref, g1_ref, wq_ref, wk_ref, wv_ref, wf_ref, bf_ref, gq_ref, gk_ref,
                    q_ref, k_ref, v_ref, lf_ref, *rest, emit_bf16_kv):
    if emit_bf16_kv:
        kb_ref, vb_ref, h_scr = rest
    else:
        (h_scr,) = rest
    j = pl.program_id(1)

    @pl.when(j == 0)
    def _():
        x = x_ref[...]
        ms = jnp.mean(x * x, axis=-1, keepdims=True)
        h_scr[...] = (x * lax.rsqrt(ms + EPS) * g1_ref[...]).astype(bf16)
        lf_ref[...] = _log_sigmoid(_dot(h_scr[...], wf_ref[...]) + bf_ref[...])

    h = h_scr[...]
    zq = _dot(h, wq_ref[...])
    zk = _dot(h, wk_ref[...])
    zv = _dot(h, wv_ref[...])
    v_ref[...] = zv
    if emit_bf16_kv:
        vb_ref[...] = zv.astype(bf16)

    def store(q, k):
        q_ref[...] = q.astype(q_ref.dtype)
        k_ref[...] = k
        if emit_bf16_kv:
            kb_ref[...] = k.astype(bf16)

    @pl.when(j < H_SB // 2)
    def _():
        store(zq * (LOG2E * HEAD_DIM ** -0.5), zk)

    @pl.when((j >= H_SB // 2) & (j < (H_SB + H_FOX) // 2))
    def _():
        store(_norm_heads(zq, gq_ref[...], HEAD_DIM) * (LOG2E * HEAD_DIM ** -0.5),
              _norm_heads(zk, gk_ref[...], HEAD_DIM))

    @pl.when(j >= (H_SB + H_FOX) // 2)
    def _():
        store(_norm_heads(zq, gq_ref[...], DIFF_QK) * (LOG2E * DIFF_QK ** -0.5),
              _norm_heads(zk, gk_ref[...], DIFF_QK))


def _in_proj(x, g1, w_in, wf, bfp, gq, gk, *, layer, q_dtype, emit_bf16_kv):
    m, d = x.shape
    tm = min(1024, m)
    tn = 2 * HEAD_DIM
    n_tiles = MIX_W // tn
    row = lambda i, j: (i, 0)
    col = lambda i, j: (0, j)
    tile = lambda i, j: (i, j)
    const = lambda i, j: (0, 0)
    w_spec = lambda part: pl.BlockSpec((None, d, tn), lambda i, j: (layer, 0, part * n_tiles + j))
    out_shape = [jax.ShapeDtypeStruct((m, MIX_W), q_dtype),
                 jax.ShapeDtypeStruct((m, MIX_W), f32),
                 jax.ShapeDtypeStruct((m, MIX_W), f32),
                 jax.ShapeDtypeStruct((m, HEAD_DIM), f32)]
    out_specs = [pl.BlockSpec((tm, tn), tile), pl.BlockSpec((tm, tn), tile), pl.BlockSpec((tm, tn), tile),
                 pl.BlockSpec((tm, HEAD_DIM), row)]
    if emit_bf16_kv:
        out_shape += [jax.ShapeDtypeStruct((m, MIX_W), bf16)] * 2
        out_specs += [pl.BlockSpec((tm, tn), tile)] * 2
    return pl.pallas_call(
        functools.partial(_in_proj_kernel, emit_bf16_kv=emit_bf16_kv),
        grid=(m // tm, n_tiles),
        in_specs=[pl.BlockSpec((tm, d), row), pl.BlockSpec((1, d), const),
                  w_spec(0), w_spec(1), w_spec(2),
                  pl.BlockSpec((d, HEAD_DIM), const), pl.BlockSpec((1, HEAD_DIM), const),
                  pl.BlockSpec((1, tn), col), pl.BlockSpec((1, tn), col)],
        out_specs=out_specs,
        out_shape=out_shape,
        scratch_shapes=[pltpu.VMEM((tm, d), bf16)],
        compiler_params=pltpu.CompilerParams(dimension_semantics=("arbitrary", "arbitrary"),
                                             vmem_limit_bytes=VMEM_LIMIT),
        name="in_proj",
    )(x, g1, w_in, w_in, w_in, wf, bfp, gq, gk)


def _cumsum_kernel(lf_ref, ft_ref, carry_scr):
    @pl.when(pl.program_id(1) == 0)
    def _():
        carry_scr[...] = jnp.zeros_like(carry_scr)

    tc = lf_ref.shape[0]
    lf_t = lf_ref[...].T[:8]
    r = lax.broadcasted_iota(jnp.int32, (tc, tc), 0)
    c = lax.broadcasted_iota(jnp.int32, (tc, tc), 1)
    cs = _dot_exact(lf_t, (r <= c).astype(f32)) + carry_scr[:, 0:1]
    ft_ref[...] = cs
    carry_scr[...] = jnp.broadcast_to(cs[:, tc - 1:tc], carry_scr.shape)


def _cumsum_t(lf, batch, t):
    tc = min(512, t)
    nblk = t // tc
    return pl.pallas_call(
        _cumsum_kernel,
        grid=(batch, nblk),
        in_specs=[pl.BlockSpec((tc, HEAD_DIM), lambda b, i: (b * nblk + i, 0))],
        out_specs=pl.BlockSpec((None, 8, tc), lambda b, i: (b, 0, i)),
        out_shape=jax.ShapeDtypeStruct((batch, 8, t), f32),
        scratch_shapes=[pltpu.VMEM((8, HEAD_DIM), f32)],
        compiler_params=pltpu.CompilerParams(dimension_semantics=("arbitrary", "arbitrary")),
        name="logf_cumsum",
    )(lf)


def _softmax_block(s, v1, m, accl):
    m_new = jnp.maximum(m, jnp.max(s, axis=1, keepdims=True))
    alpha = jnp.exp2(m - m_new)
    p = jnp.exp2(s - m_new)
    return m_new, alpha * accl + _dot(p.astype(bf16), v1)


def _prompt_attn_kernel(q_ref, k_ref, v_ref, ft_ref, lam_ref, gsub_ref, o_ref, *, tq, lam_init):
    h = pl.program_id(1)
    qi = pl.program_id(2)
    q = q_ref[...]
    row = lax.broadcasted_iota(jnp.int32, (tq, tq), 0)
    col = lax.broadcasted_iota(jnp.int32, (tq, tq), 1)

    @pl.when(h < H_SB)
    def _():
        suffix = _suffix_matrix()

        def block(kb, carry, acc, masked):
            off = pl.multiple_of(kb * tq, tq)
            z = _dot_nt(q, k_ref[pl.ds(off, tq), :])
            w, carry = _sb_weights(z, (col < row) if masked else None, carry, suffix)
            return carry, acc + _dot(w.astype(bf16), v_ref[pl.ds(off, tq), :])

        carry, acc = block(qi, jnp.zeros((tq, LANES), f32), jnp.zeros((tq, HEAD_DIM), f32), True)

        def cond(st):
            return (st[0] >= 0) & (jnp.min(st[1]) < SB_CUTOFF)

        def body(st):
            carry, acc = block(st[0], st[1], st[2], False)
            return st[0] - 1, carry, acc

        _, _, acc = lax.while_loop(cond, body, (qi - 1, carry, acc))
        o_ref[...] = acc.astype(o_ref.dtype)

    def causal_softmax(qm, bias, mask):
        rows = qm.shape[0]

        def scores(kb):
            return _dot_nt(qm, k_ref[pl.ds(pl.multiple_of(kb * tq, tq), tq), :])

        def update(kb, s_raw, m, accl, masked):
            off = pl.multiple_of(kb * tq, tq)
            v1 = jnp.concatenate([v_ref[pl.ds(off, tq), :], jnp.ones((tq, HEAD_DIM), bf16)], axis=1)
            s = s_raw + bias(kb, off)
            if masked:
                s = jnp.where(mask, s, NEG)
            return _softmax_block(s, v1, m, accl)

        def body(kb, st):
            s_next = scores(kb + 1)
            m, accl = update(kb, st[0], st[1], st[2], False)
            return s_next, m, accl

        st = (scores(0), jnp.full((rows, 1), NEG, f32), jnp.zeros((rows, 2 * HEAD_DIM), f32))
        st = lax.fori_loop(0, qi, body, st)
        _, accl = update(qi, st[0], st[1], st[2], True)
        return accl[:, :HEAD_DIM] / accl[:, HEAD_DIM:]

    @pl.when((h >= H_SB) & (h < H_SB + H_FOX))
    def _():
        fx = h - H_SB
        o = causal_softmax(q, lambda kb, off: -LOG2E * ft_ref[pl.ds(fx, 1), pl.ds(off, tq)], col <= row)
        o_ref[...] = o.astype(o_ref.dtype)

    @pl.when(h >= H_SB + H_FOX)
    def _():
        hd = h - (H_SB + H_FOX)
        slope = LOG2E * jnp.exp2(jnp.full((1, 1), -ALIBI_RATE, f32) * (hd + 1).astype(f32))
        lane = lax.broadcasted_iota(jnp.int32, (tq, HEAD_DIM), 1)
        zero = jnp.zeros_like(q)
        q2 = jnp.concatenate([jnp.where(lane < DIFF_QK, q, zero), jnp.where(lane < DIFF_QK, zero, q)], axis=0)
        row2 = lax.broadcasted_iota(jnp.int32, (2 * tq, tq), 0) & (tq - 1)
        col2 = lax.broadcasted_iota(jnp.int32, (2 * tq, tq), 1)
        kidx = lax.broadcasted_iota(jnp.int32, (1, tq), 1).astype(f32)
        o = causal_softmax(q2, lambda kb, off: slope * (kidx + ((kb - qi) * tq).astype(f32)), col2 <= row2)
        out = _diff_finish(o[:tq], o[tq:], _lam(lam_ref, lam_init), gsub_ref[...], lam_init)
        o_ref[...] = out.astype(o_ref.dtype)


def _prompt_attn(q, k, v, ft, lam_p, gsub, *, batch, t, lam_init):
    tq = min(512, t)
    nq = t // tq
    full = lambda b, h, i: (0, 0)
    return pl.pallas_call(
        functools.partial(_prompt_attn_kernel, tq=tq, lam_init=lam_init),
        grid=(batch, N_HEADS, nq),
        in_specs=[pl.BlockSpec((tq, HEAD_DIM), lambda b, h, i: (b * nq + i, h)),
                  pl.BlockSpec((t, HEAD_DIM), lambda b, h, i: (b, h)),
                  pl.BlockSpec((t, HEAD_DIM), lambda b, h, i: (b, h)),
                  pl.BlockSpec((None, 8, t), lambda b, h, i: (b, 0, 0)),
                  pl.BlockSpec(lam_p.shape, full), pl.BlockSpec(gsub.shape, full)],
        out_specs=pl.BlockSpec((tq, HEAD_DIM), lambda b, h, i: (b * nq + i, h)),
        out_shape=jax.ShapeDtypeStruct((batch * t, MIX_W), bf16),
        compiler_params=pltpu.CompilerParams(dimension_semantics=("arbitrary", "arbitrary", "arbitrary"),
                                             vmem_limit_bytes=VMEM_LIMIT),
        name="prompt_attn",
    )(q, k, v, ft, lam_p, gsub)


def _head_rows(ref, r):
    keys = ref.shape[0]
    return ref.reshape(keys * 8, HEAD_DIM)[pl.ds(r, keys, stride=8), :]


def _chunk_rows(c):
    h0 = 2 * c
    if h0 < H_SB + H_FOX:
        return h0 * DEC_Q, DEC_Q
    return DIFF_ROW0 + (h0 - H_SB - H_FOX) * 2 * DEC_Q, 2 * DEC_Q


def _sample_attn_kernel(pt_ref, q_ref, kn_ref, vn_ref, lfn_ref, *rest, pages_per_step, n_groups, past_len,
                        lam_init):
    pp = pages_per_step
    k_refs = rest[:2 * pp]
    v_refs = rest[2 * pp:4 * pp]
    lf_refs = rest[4 * pp:5 * pp]
    lam_ref, gsub_ref, o_ref, qbd_scr, acc_scr, m_scr, l_scr, hc_scr = rest[5 * pp:]
    g = pl.program_id(1)
    page = lf_refs[0].shape[1]

    r_c = lax.broadcasted_iota(jnp.int32, (page, page), 0)
    c_c = lax.broadcasted_iota(jnp.int32, (page, page), 1)
    after_f32 = jnp.where(r_c > c_c, 1.0, 0.0)
    suffix = _suffix_matrix()
    head = lax.broadcasted_iota(jnp.int32, (DIFF_ROWS, 1), 0) >> 4
    slope = LOG2E * jnp.exp2(-ALIBI_RATE * (head + 1).astype(f32))

    def qidx(rows):
        return lax.broadcasted_iota(jnp.int32, (rows, 1), 0) & (DEC_Q - 1)

    def process(load_k, load_v, lfs, kpos0s, is_new):
        n_p = len(lfs)
        n = n_p * page
        col = lax.broadcasted_iota(jnp.int32, (1, n), 1)

        def chunk(load, c):
            rows = [jnp.concatenate([load(p, 2 * c), load(p, 2 * c + 1)], axis=1) for p in range(n_p)]
            return (jnp.concatenate(rows, axis=0) if n_p > 1 else rows[0]).astype(bf16)

        scores = []
        for c in range(HALF_HEADS):
            r0, per = _chunk_rows(c)
            scores.append(_dot_nt(qbd_scr[r0:r0 + 2 * per, :], chunk(load_k, c)))
        s_all = jnp.concatenate(scores, axis=0)

        sb_ok = (col < qidx(SB_ROWS)) if is_new else None
        w, carry = _sb_weights(s_all[:SB_ROWS], sb_ok, l_scr[:SB_ROWS], suffix)
        l_scr[:SB_ROWS] = carry

        later = [None] * n_p
        hc = hc_scr[:, 0:1]
        for p in reversed(range(n_p)):
            later[p] = _dot_exact(lfs[p], after_f32) + hc
            hc = hc + jnp.sum(lfs[p], axis=1, keepdims=True)
        hc_scr[...] = jnp.broadcast_to(hc, hc_scr.shape)
        later = jnp.concatenate(later, axis=1) if n_p > 1 else later[0]
        later_rows = jnp.concatenate(
            [jnp.broadcast_to(later[f:f + 1], (DEC_Q, n)) for f in range(H_FOX)], axis=0)
        s_fx = s_all[FOX_ROW0:DIFF_ROW0] + LOG2E * later_rows

        kpos = [lax.broadcasted_iota(jnp.int32, (1, page), 1).astype(f32) + kpos0s[p] for p in range(n_p)]
        kpos = jnp.concatenate(kpos, axis=1) if n_p > 1 else kpos[0]
        s_df = s_all[DIFF_ROW0:] + slope * kpos

        s = jnp.concatenate([s_fx, s_df], axis=0)
        if is_new:
            s = jnp.where(col <= qidx(FOX_ROWS + DIFF_ROWS), s, NEG)
        m_old = m_scr[FOX_ROW0:]
        m_new = jnp.maximum(m_old, jnp.max(s, axis=1, keepdims=True))
        alpha = jnp.exp2(m_old - m_new)
        p_sm = jnp.exp2(s - m_new)
        m_scr[FOX_ROW0:] = m_new
        l_scr[FOX_ROW0:] = alpha * l_scr[FOX_ROW0:] + jnp.sum(p_sm, axis=1, keepdims=True)

        pw = jnp.concatenate([w, p_sm], axis=0).astype(bf16)
        pieces = []
        for c in range(HALF_HEADS):
            r0, per = _chunk_rows(c)
            res = _dot(pw[r0:r0 + 2 * per], chunk(load_v, c))
            pieces.append(res[:per, :HEAD_DIM])
            pieces.append(res[per:, HEAD_DIM:])
        pv = jnp.concatenate(pieces, axis=0)
        scale = jnp.concatenate([jnp.ones((SB_ROWS, 1), f32), alpha], axis=0)
        acc_scr[...] = scale * acc_scr[...] + pv

    @pl.when(g == 0)
    def _():
        q = q_ref[...]
        lane = lax.broadcasted_iota(jnp.int32, (DEC_Q, HEAD_DIM), 1)
        zero = jnp.zeros((DEC_Q, HEAD_DIM), f32)
        rows = []
        for hh in range(N_HEADS):
            tile = q[:, hh * HEAD_DIM:(hh + 1) * HEAD_DIM]
            if hh < H_SB + H_FOX:
                tiles = [tile]
            else:
                tiles = [jnp.where(lane < DIFF_QK, tile, 0.0), jnp.where(lane < DIFF_QK, 0.0, tile)]
            for tl in tiles:
                rows.append(jnp.concatenate([tl, zero] if hh % 2 == 0 else [zero, tl], axis=1))
        qbd_scr[...] = jnp.concatenate(rows, axis=0).astype(bf16)
        acc_scr[...] = jnp.zeros_like(acc_scr)
        l_scr[...] = jnp.zeros_like(l_scr)
        hc_scr[...] = jnp.zeros_like(hc_scr)
        m_scr[...] = jnp.full(m_scr.shape, NEG, f32)
        pad = jnp.zeros((page - DEC_Q, HEAD_DIM), f32)
        process(lambda p, hh: jnp.concatenate([kn_ref[:, hh * HEAD_DIM:(hh + 1) * HEAD_DIM], pad], axis=0),
                lambda p, hh: jnp.concatenate([vn_ref[:, hh * HEAD_DIM:(hh + 1) * HEAD_DIM], pad], axis=0),
                [lfn_ref[...]], [0.0], True)

    def load_from(refs):
        return lambda p, hh: _head_rows(refs[2 * p + hh // 8], hh % 8)

    first_page = (n_groups - 1 - g) * pp
    process(load_from(k_refs), load_from(v_refs), [lf_refs[p][...] for p in range(pp)],
            [((first_page + p) * page - past_len).astype(f32) for p in range(pp)], False)

    @pl.when(g == n_groups - 1)
    def _():
        acc = acc_scr[...]
        l = l_scr[...]
        lam = _lam(lam_ref, lam_init)
        gsub = gsub_ref[...]
        outs = []
        for hh in range(N_HEADS):
            if hh < H_SB:
                outs.append(acc[hh * DEC_Q:(hh + 1) * DEC_Q])
            elif hh < H_SB + H_FOX:
                r0 = hh * DEC_Q
                outs.append(acc[r0:r0 + DEC_Q] / l[r0:r0 + DEC_Q])
            else:
                r0 = DIFF_ROW0 + (hh - H_SB - H_FOX) * 2 * DEC_Q
                o1 = acc[r0:r0 + DEC_Q] / l[r0:r0 + DEC_Q]
                o2 = acc[r0 + DEC_Q:r0 + 2 * DEC_Q] / l[r0 + DEC_Q:r0 + 2 * DEC_Q]
                outs.append(_diff_finish(o1, o2, lam, gsub, lam_init))
        o_ref[...] = jnp.concatenate(outs, axis=1)


def _sample_attn(page_table, q, kn, vn, lfn, cache_k, cache_v, cache_lf, lam_p, gsub, *, layer, lam_init):
    n_seq, n_pages = page_table.shape
    page = cache_lf.shape[3]
    pp = math.gcd(n_pages, MAX_PAGES_PER_STEP)
    n_groups = n_pages // pp
    assert q.shape[1] == DEC_Q and page == LANES

    def page_index(b, g, pt, i):
        return pt[b, (n_groups - 1 - g) * pp + i]

    def half_map(i, half):
        return lambda b, g, pt: (layer, page_index(b, g, pt, i), 0, half, 0, 0)

    def lf_map(i):
        return lambda b, g, pt: (layer, page_index(b, g, pt, i), 0, 0)

    seq = lambda b, g, pt: (b, 0, 0)
    full = lambda b, g, pt: (0, 0)
    half_block = (None, None, page, None, 8, HEAD_DIM)
    in_specs = [pl.BlockSpec((None, DEC_Q, MIX_W), seq), pl.BlockSpec((None, DEC_Q, MIX_W), seq),
                pl.BlockSpec((None, DEC_Q, MIX_W), seq), pl.BlockSpec((None, 8, page), seq)]
    for _ in range(2):
        in_specs += [pl.BlockSpec(half_block, half_map(i, half)) for i in range(pp) for half in range(2)]
    in_specs += [pl.BlockSpec((None, None, 8, page), lf_map(i)) for i in range(pp)]
    in_specs += [pl.BlockSpec(lam_p.shape, full), pl.BlockSpec(gsub.shape, full)]
    grid_spec = pltpu.PrefetchScalarGridSpec(
        num_scalar_prefetch=1,
        grid=(n_seq, n_groups),
        in_specs=in_specs,
        out_specs=pl.BlockSpec((None, DEC_Q, MIX_W), seq),
        scratch_shapes=[pltpu.VMEM((ALL_ROWS, 2 * HEAD_DIM), bf16), pltpu.VMEM((ALL_ROWS, HEAD_DIM), f32),
                        pltpu.VMEM((ALL_ROWS, 1), f32), pltpu.VMEM((ALL_ROWS, LANES), f32),
                        pltpu.VMEM((8, LANES), f32)],
    )
    return pl.pallas_call(
        functools.partial(_sample_attn_kernel, pages_per_step=pp, n_groups=n_groups,
                          past_len=n_pages * page, lam_init=lam_init),
        grid_spec=grid_spec,
        out_shape=jax.ShapeDtypeStruct((n_seq, DEC_Q, MIX_W), f32),
        compiler_params=pltpu.CompilerParams(dimension_semantics=("arbitrary", "arbitrary"),
                                             vmem_limit_bytes=VMEM_LIMIT),
        name="sample_attn",
    )(page_table, q, kn, vn, lfn, *([cache_k] * (2 * pp)), *([cache_v] * (2 * pp)), *([cache_lf] * pp),
      lam_p, gsub)


def _out_proj_kernel(o_ref, w_ref, x_ref, y_ref):
    y_ref[...] = x_ref[...] + _dot(o_ref[...].astype(bf16), w_ref[...])


def _out_proj(o, w, x):
    m, d = x.shape
    tm = min(512, m)
    tn = min(1024, d)
    return pl.pallas_call(
        _out_proj_kernel,
        grid=(m // tm, d // tn),
        in_specs=[pl.BlockSpec((tm, o.shape[1]), lambda i, j: (i, 0)),
                  pl.BlockSpec((w.shape[0], tn), lambda i, j: (0, j)),
                  pl.BlockSpec((tm, tn), lambda i, j: (i, j))],
        out_specs=pl.BlockSpec((tm, tn), lambda i, j: (i, j)),
        out_shape=jax.ShapeDtypeStruct((m, d), f32),
        compiler_params=pltpu.CompilerParams(dimension_semantics=("arbitrary", "arbitrary"),
                                             vmem_limit_bytes=VMEM_LIMIT),
        name="out_proj",
    )(o, w, x)


def _mlp_kernel(x_ref, g2_ref, wu_ref, wd_ref, y_ref, h_scr):
    j = pl.program_id(1)

    @pl.when(j == 0)
    def _():
        x = x_ref[...]
        ms = jnp.mean(x * x, axis=-1, keepdims=True)
        h_scr[...] = (x * lax.rsqrt(ms + EPS) * g2_ref[...]).astype(bf16)
        y_ref[...] = x

    u = jnp.maximum(_dot(h_scr[...], wu_ref[...]), 0.0)
    y_ref[...] += _dot((u * u).astype(bf16), wd_ref[...])


def _mlp(x, g2, wu, wd):
    m, d = x.shape
    ff = wu.shape[1]
    tm = min(512, m)
    tf = min(1024, ff)
    return pl.pallas_call(
        _mlp_kernel,
        grid=(m // tm, ff // tf),
        in_specs=[pl.BlockSpec((tm, d), lambda i, j: (i, 0)), pl.BlockSpec((1, d), lambda i, j: (0, 0)),
                  pl.BlockSpec((d, tf), lambda i, j: (0, j)), pl.BlockSpec((tf, d), lambda i, j: (j, 0))],
        out_specs=pl.BlockSpec((tm, d), lambda i, j: (i, 0)),
        out_shape=jax.ShapeDtypeStruct((m, d), f32),
        scratch_shapes=[pltpu.VMEM((tm, d), bf16)],
        compiler_params=pltpu.CompilerParams(dimension_semantics=("arbitrary", "arbitrary"),
                                             vmem_limit_bytes=VMEM_LIMIT),
        name="mlp",
    )(x, g2, wu, wd)


def kernel(x_prompt, x_sample, cache_k, cache_v, cache_logf, page_table, ln1_g, w_in, b_f, g_q_fox, g_k_fox,
           g_q_diff, g_k_diff, diff_lambda, g_sub_diff, w_out, ln2_g, w_up, w_down):
    batch, t, d = x_prompt.shape
    n_seq, dec_q, _ = x_sample.shape
    depth, n_phys, page = cache_k.shape[:3]

    xp = x_prompt.reshape(batch * t, d)
    xs = x_sample.reshape(n_seq * dec_q, d)
    ck = cache_k.reshape(depth, n_phys, page, 2, 8, HEAD_DIM)
    cv = cache_v.reshape(depth, n_phys, page, 2, 8, HEAD_DIM)
    clf = jnp.pad(cache_logf.astype(f32).transpose(0, 1, 3, 2), ((0, 0), (0, 0), (0, 8 - H_FOX), (0, 0)))

    ones_sb = jnp.ones((H_SB * HEAD_DIM,), f32)
    w_in_bf = w_in.astype(bf16)
    outs = {name: [] for name in ("kp", "vp", "fp", "ks", "vs", "fs")}
    for l in range(depth):
        lam_init = 0.8 - 0.6 * math.exp(-0.3 * l)
        wf = jnp.pad(w_in[l, :, 3 * MIX_W:], ((0, 0), (0, HEAD_DIM - H_FOX))).astype(bf16)
        bfp = jnp.pad(b_f[l].astype(f32), (0, HEAD_DIM - H_FOX)).reshape(1, HEAD_DIM)
        gq = jnp.concatenate([ones_sb, jnp.tile(g_q_fox[l], H_FOX), jnp.tile(g_q_diff[l].reshape(-1), H_DIFF)])
        gk = jnp.concatenate([ones_sb, jnp.tile(g_k_fox[l], H_FOX), jnp.tile(g_k_diff[l].reshape(-1), H_DIFF)])
        gq = gq.astype(f32).reshape(1, MIX_W)
        gk = gk.astype(f32).reshape(1, MIX_W)
        g1 = ln1_g[l].astype(f32).reshape(1, d)
        g2 = ln2_g[l].astype(f32).reshape(1, d)
        lam_p = diff_lambda[l].astype(f32)
        gsub = g_sub_diff[l].astype(f32).reshape(1, HEAD_DIM)
        wo = w_out[l].astype(bf16)
        wu = w_up[l].astype(bf16)
        wd = w_down[l].astype(bf16)

        q, k, v, lf, kb, vb = _in_proj(xp, g1, w_in_bf, wf, bfp, gq, gk, layer=l, q_dtype=bf16, emit_bf16_kv=True)
        ft = _cumsum_t(lf, batch, t)
        o = _prompt_attn(q, kb, vb, ft, lam_p, gsub, batch=batch, t=t, lam_init=lam_init)
        xp = _mlp(_out_proj(o, wo, xp), g2, wu, wd)
        outs["kp"].append(k.reshape(batch, t, N_HEADS, HEAD_DIM))
        outs["vp"].append(v.reshape(batch, t, N_HEADS, HEAD_DIM))
        outs["fp"].append(lf[:, :H_FOX].reshape(batch, t, H_FOX))

        q, k, v, lf = _in_proj(xs, g1, w_in_bf, wf, bfp, gq, gk, layer=l, q_dtype=f32, emit_bf16_kv=False)
        lfn = lf.reshape(n_seq, dec_q, HEAD_DIM)[:, :, :8].transpose(0, 2, 1)
        lfn = jnp.pad(lfn, ((0, 0), (0, 0), (0, page - dec_q)))
        o = _sample_attn(page_table, q.reshape(n_seq, dec_q, MIX_W), k.reshape(n_seq, dec_q, MIX_W),
                         v.reshape(n_seq, dec_q, MIX_W), lfn, ck, cv, clf, lam_p, gsub,
                         layer=l, lam_init=lam_init)
        xs = _mlp(_out_proj(o.reshape(n_seq * dec_q, MIX_W), wo, xs), g2, wu, wd)
        outs["ks"].append(k.reshape(n_seq, dec_q, N_HEADS, HEAD_DIM))
        outs["vs"].append(v.reshape(n_seq, dec_q, N_HEADS, HEAD_DIM))
        outs["fs"].append(lf[:, :H_FOX].reshape(n_seq, dec_q, H_FOX))

    return (xp.reshape(batch, t, d), xs.reshape(n_seq, dec_q, d),
            jnp.stack(outs["kp"]), jnp.stack(outs["vp"]), jnp.stack(outs["fp"]),
            jnp.stack(outs["ks"]), jnp.stack(outs["vs"]), jnp.stack(outs["fs"]))
```
